```python
import jax, jax.numpy as jnp
from jax import lax
import numpy as np

D_MODEL = 1024
BATCH = 8
SEQ = 2048
DEPTH = 4
DEC_BATCH = 128
DEC_SEQ = 1
PAST_LEN = 16384
PAGE_SIZE = 128

CHUNK = 128
D_A = 512
G_A = 4
GA_DIM = D_A // G_A
D_B = 512
HEAD_B = 64
H_B = D_B // HEAD_B
LORA_W = 64
LORA_A = 64
LORA_G = 128
C_A = 2 * D_A
C_RWKV = 3 * D_B + LORA_W + LORA_A + LORA_G
C_GATE = 2 * D_MODEL
C_IN = C_A + C_RWKV + C_GATE
N_GROUPS = 4
EXPERTS_PER_GROUP = 4
N_EXPERTS = N_GROUPS * EXPERTS_PER_GROUP
TOP_K = 2
D_EXPERT = 256
PLE_DIM = 256
DN_ALPHA = (2 * DEPTH) ** 0.25
DN_BETA = (8 * DEPTH) ** -0.25
LN_EPS = 1e-5
GN_EPS = 64e-5

kernel_name = 'hybrid_gmlp_rwkv7_hmoe_decode_step'

F32 = jnp.float32


def layer_norm(x, g, b, eps=LN_EPS):
    xf = x.astype(F32)
    mu = jnp.mean(xf, axis=-1, keepdims=True)
    var = jnp.mean(jnp.square(xf - mu), axis=-1, keepdims=True)
    return ((xf - mu) * lax.rsqrt(var + eps) * g.astype(F32) + b.astype(F32)).astype(x.dtype)


def chunk_spatial_gate(v, w_s, b_s):
    bt, t, _ = v.shape
    n_chunks = -(-t // CHUNK)
    vp = jnp.pad(v, ((0, 0), (0, n_chunks * CHUNK - t), (0, 0)))
    vp = vp.reshape(bt, n_chunks, CHUNK, G_A, GA_DIM)
    causal = jnp.tril(jnp.ones((CHUNK, CHUNK), dtype=bool))
    w = jnp.where(causal[None], w_s, jnp.zeros_like(w_s))
    out = jnp.einsum('gts,bcsge->bctge', w, vp) + jnp.transpose(b_s)[None, None, :, :, None]
    return out.reshape(bt, n_chunks * CHUNK, D_A)[:, :t]


def rwkv7_recurrence(s0, r, decay, k, v, a_vec, b_vec):
    def step(s, inp):
        r_t, w_t, k_t, v_t, a_t, b_t = inp
        sa = jnp.einsum('bhij,bhj->bhi', s, a_t)
        s = s * w_t[:, :, None, :] + sa[..., None] * b_t[:, :, None, :] + v_t[..., None] * k_t[:, :, None, :]
        return s, jnp.einsum('bhij,bhj->bhi', s, r_t)
    xs = tuple(jnp.swapaxes(z.astype(F32), 0, 1) for z in (r, decay, k, v, a_vec, b_vec))
    s_final, ys = lax.scan(step, s0.astype(F32), xs)
    return jnp.swapaxes(ys, 0, 1), s_final


def hier_moe(x, lw):
    bt, t, d = x.shape
    xt = x.reshape(bt * t, d)
    pg = jax.nn.softmax((xt @ lw['w_rg'] + lw['b_rg']).astype(F32), axis=-1)
    gp, gi = lax.top_k(pg, 1)
    le = (jnp.einsum('td,gde->tge', xt, lw['w_re']) + lw['b_re']).astype(F32)
    le = jnp.sum(le * jax.nn.one_hot(gi[:, 0], N_GROUPS, dtype=F32)[:, :, None], axis=1)
    ev, ei = lax.top_k(le, TOP_K)
    ew = jax.nn.softmax(ev, axis=-1) * gp
    eid = gi * EXPERTS_PER_GROUP + ei
    comb = jnp.sum(jax.nn.one_hot(eid, N_EXPERTS, dtype=F32) * ew[..., None], axis=1)
    h = jax.nn.silu(jnp.einsum('td,edf->tef', xt, lw['w_e_gate'])) * jnp.einsum('td,edf->tef', xt, lw['w_e_up'])
    h = h * comb[:, :, None].astype(h.dtype)
    return jnp.einsum('tef,efd->td', h, lw['w_e_down']).reshape(bt, t, d)


def heads(z, bt, t):
    return z.reshape(bt, t, H_B, HEAD_B)


def trunk_layer(x, pe, s0, shift0, lw):
    bt, t, _ = x.shape
    proj = x @ lw['w_in']
    pa = proj[..., :C_A]
    pb = proj[..., C_A:C_A + C_RWKV]
    gates = jax.nn.sigmoid(proj[..., C_A + C_RWKV:] + lw['b_gate'])
    gate_a, gate_b = gates[..., :D_MODEL], gates[..., D_MODEL:]

    z = jax.nn.gelu(pa)
    u = z[..., :D_A]
    va = layer_norm(z[..., D_A:], lw['ln_v_g'], lw['ln_v_b'])
    ya = u * chunk_spatial_gate(va, lw['w_s'], lw['b_s'])
    v_rows = va[:, ((t - 1) // CHUNK) * CHUNK:]

    prev = jnp.concatenate([shift0[:, None, :].astype(pb.dtype), pb[:, :-1]], axis=1)
    xb = pb + lw['mu_shift'] * (prev - pb)
    new_shift = pb[:, -1]
    r = xb[..., 0:D_B]
    k = xb[..., D_B:2 * D_B]
    v = xb[..., 2 * D_B:3 * D_B]
    o = 3 * D_B
    wd = xb[..., o:o + LORA_W]
    ad = xb[..., o + LORA_W:o + LORA_W + LORA_A]
    gd = xb[..., o + LORA_W + LORA_A:]
    w_log = -jax.nn.softplus(-(lw['w0'] + jnp.tanh(wd) @ lw['w_w2'])) - 0.5
    decay = jnp.exp(-jnp.exp(w_log.astype(F32)))
    iclr = jax.nn.sigmoid(lw['a0'] + ad @ lw['w_a2'])
    g = jax.nn.sigmoid(gd) @ lw['w_g2']
    kk = heads(k * lw['k_k'], bt, t).astype(F32)
    kk = kk / jnp.maximum(jnp.sqrt(jnp.sum(kk * kk, axis=-1, keepdims=True)), 1e-12)
    k = k * (1.0 + (iclr - 1.0) * lw['k_a'])
    rh, kh, vh = heads(r, bt, t), heads(k, bt, t), heads(v, bt, t)
    ah = heads(iclr, bt, t).astype(F32)
    yh, s_new = rwkv7_recurrence(s0, rh, heads(decay, bt, t), kh, vh, -kk, kk * ah)
    yh = layer_norm(yh, lw['lnx_g'].reshape(H_B, HEAD_B), lw['lnx_b'].reshape(H_B, HEAD_B), GN_EPS)
    bonus = jnp.sum(rh.astype(F32) * kh.astype(F32) * lw['r_k'].astype(F32), axis=-1, keepdims=True) * vh.astype(F32)
    yb = ((yh + bonus).reshape(bt, t, D_B) * g.astype(F32)).astype(x.dtype)

    mix = (gate_a * (ya @ lw['w_a_out']) + gate_b * (yb @ lw['w_b_out'])) @ lw['w_o']
    x = layer_norm(DN_ALPHA * x + mix, lw['ln1_g'], lw['ln1_b'])

    moe = hier_moe(x, lw)
    ple = jax.nn.sigmoid(x @ lw['w_pe_gate']) * (pe @ lw['w_pe_proj'])
    x = layer_norm(DN_ALPHA * x + moe + ple, lw['ln2_g'], lw['ln2_b'])
    return x, s_new.astype(x.dtype), new_shift, v_rows


def setup_inputs(seed: int = 0) -> dict:
    key = jax.random.key(seed)
    ks = iter(jax.random.split(key, 48))
    L = DEPTH

    def nrm(shape, scale):
        return jax.random.normal(next(ks), shape, F32) * scale

    def unif(shape, lo, hi):
        return jax.random.uniform(next(ks), shape, F32, lo, hi)

    return {
        'x_prompt': nrm((BATCH, SEQ, D_MODEL), 1.0),
        'x_sample': nrm((DEC_BATCH, DEC_SEQ, D_MODEL), 1.0),
        'state_rwkv': nrm((DEPTH, DEC_BATCH, H_B, HEAD_B, HEAD_B), 0.5),
        'state_shift': nrm((DEPTH, DEC_BATCH, C_RWKV), 1.0),
        'p_prompt': nrm((DEPTH, BATCH, SEQ, PLE_DIM), 1.0),
        'p_sample': nrm((DEPTH, DEC_BATCH, DEC_SEQ, PLE_DIM), 1.0),
        'ln_in_g': 1.0 + nrm((D_MODEL,), 0.02),
        'ln_in_b': nrm((D_MODEL,), 0.02),
        'w_in': nrm((L, D_MODEL, C_IN), D_MODEL ** -0.5),
        'b_gate': nrm((L, C_GATE), 0.1),
        'ln_v_g': 1.0 + nrm((L, D_A), 0.02),
        'ln_v_b': nrm((L, D_A), 0.02),
        'w_s': nrm((L, G_A, CHUNK, CHUNK), CHUNK ** -0.5),
        'b_s': 1.0 + nrm((L, G_A, CHUNK), 0.1),
        'w_a_out': nrm((L, D_A, D_MODEL), D_A ** -0.5),
        'mu_shift': unif((L, C_RWKV), 0.0, 1.0),
        'w0': unif((L, D_B), -6.0, 1.0),
        'w_w2': nrm((L, LORA_W, D_B), 0.1),
        'a0': nrm((L, D_B), 0.3),
        'w_a2': nrm((L, LORA_A, D_B), 0.1),
        'w_g2': nrm((L, LORA_G, D_B), LORA_G ** -0.5),
        'k_k': 0.85 + nrm((L, D_B), 0.05),
        'k_a': 1.0 + nrm((L, D_B), 0.05),
        'r_k': nrm((L, H_B, HEAD_B), 0.1),
        'lnx_g': 1.0 + nrm((L, D_B), 0.02),
        'lnx_b': nrm((L, D_B), 0.02),
        'w_b_out': nrm((L, D_B, D_MODEL), D_B ** -0.5),
        'w_o': nrm((L, D_MODEL, D_MODEL), DN_BETA * D_MODEL ** -0.5),
        'ln1_g': 1.0 + nrm((L, D_MODEL), 0.02),
        'ln1_b': nrm((L, D_MODEL), 0.02),
        'w_rg': nrm((L, D_MODEL, N_GROUPS), D_MODEL ** -0.5),
        'b_rg': nrm((L, N_GROUPS), 0.01),
        'w_re': nrm((L, N_GROUPS, D_MODEL, EXPERTS_PER_GROUP), D_MODEL ** -0.5),
        'b_re': nrm((L, N_GROUPS, EXPERTS_PER_GROUP), 0.01),
        'w_e_gate': nrm((L, N_EXPERTS, D_MODEL, D_EXPERT), D_MODEL ** -0.5),
        'w_e_up': nrm((L, N_EXPERTS, D_MODEL, D_EXPERT), D_MODEL ** -0.5),
        'w_e_down': nrm((L, N_EXPERTS, D_EXPERT, D_MODEL), DN_BETA * D_EXPERT ** -0.5),
        'w_pe_gate': nrm((L, D_MODEL, D_MODEL), D_MODEL ** -0.5),
        'w_pe_proj': nrm((L, PLE_DIM, D_MODEL), DN_BETA * PLE_DIM ** -0.5),
        'ln2_g': 1.0 + nrm((L, D_MODEL), 0.02),
        'ln2_b': nrm((L, D_MODEL), 0.02),
    }


def reference(x_prompt, x_sample, state_rwkv, state_shift, p_prompt, p_sample,
              ln_in_g, ln_in_b, w_in, b_gate, ln_v_g, ln_v_b, w_s, b_s, w_a_out,
              mu_shift, w0, w_w2, a0, w_a2, w_g2, k_k, k_a, r_k, lnx_g, lnx_b, w_b_out,
              w_o, ln1_g, ln1_b, w_rg, b_rg, w_re, b_re, w_e_gate, w_e_up, w_e_down,
              w_pe_gate, w_pe_proj, ln2_g, ln2_b):
    xp = layer_norm(x_prompt, ln_in_g, ln_in_b)
    xs = layer_norm(x_sample, ln_in_g, ln_in_b)
    n_prompt = x_prompt.shape[0]
    s0_prompt = jnp.zeros((n_prompt, H_B, HEAD_B, HEAD_B), F32)
    shift0_prompt = jnp.zeros((n_prompt, C_RWKV), x_prompt.dtype)
    rp, shp, cvp, rs, shs, cvs = [], [], [], [], [], []
    for i in range(DEPTH):
        lw = {
            'w_in': w_in[i], 'b_gate': b_gate[i], 'ln_v_g': ln_v_g[i], 'ln_v_b': ln_v_b[i],
            'w_s': w_s[i], 'b_s': b_s[i], 'w_a_out': w_a_out[i], 'mu_shift': mu_shift[i],
            'w0': w0[i], 'w_w2': w_w2[i], 'a0': a0[i], 'w_a2': w_a2[i], 'w_g2': w_g2[i],
            'k_k': k_k[i], 'k_a': k_a[i], 'r_k': r_k[i], 'lnx_g': lnx_g[i], 'lnx_b': lnx_b[i],
            'w_b_out': w_b_out[i], 'w_o': w_o[i], 'ln1_g': ln1_g[i], 'ln1_b': ln1_b[i],
            'w_rg': w_rg[i], 'b_rg': b_rg[i], 'w_re': w_re[i], 'b_re': b_re[i],
            'w_e_gate': w_e_gate[i], 'w_e_up': w_e_up[i], 'w_e_down': w_e_down[i],
            'w_pe_gate': w_pe_gate[i], 'w_pe_proj': w_pe_proj[i], 'ln2_g': ln2_g[i], 'ln2_b': ln2_b[i],
        }
        xp, s_p, sh_p, cv_p = trunk_layer(xp, p_prompt[i], s0_prompt, shift0_prompt, lw)
        xs, s_s, sh_s, cv_s = trunk_layer(xs, p_sample[i], state_rwkv[i], state_shift[i], lw)
        rp.append(s_p); shp.append(sh_p); cvp.append(cv_p)
        rs.append(s_s); shs.append(sh_s); cvs.append(cv_s)
    rwkv_prompt = jnp.stack(rp)
    shift_prompt = jnp.stack(shp)
    chunk_v_prompt = jnp.stack(cvp)
    rwkv_sample = jnp.stack(rs)
    shift_sample = jnp.stack(shs)
    chunk_v_sample = jnp.stack(cvs)
    return (xp, xs, rwkv_prompt, shift_prompt, chunk_v_prompt, rwkv_sample, shift_sample, chunk_v_sample)
```

```python
import functools

import jax
import jax.numpy as jnp
from jax import lax
from jax.experimental import pallas as pl
from jax.experimental.pallas import tpu as pltpu

F32 = jnp.float32
BF16 = jnp.bfloat16

D_MODEL = 1024
CHUNK = 128
D_A = 512
G_A = 4
D_B = 512
HEAD_B = 64
H_B = D_B // HEAD_B
LORA_W = 64
LORA_A = 64
LORA_G = 128
C_A = 2 * D_A
C_RWKV = 3 * D_B + LORA_W + LORA_A + LORA_G
C_GATE = 2 * D_MODEL
C_IN = C_A + C_RWKV + C_GATE
N_GROUPS = 4
EXPERTS_PER_GROUP = 4
N_EXPERTS = N_GROUPS * EXPERTS_PER_GROUP
D_EXPERT = 256
PLE_DIM = 256
LN_EPS = 1e-5
GN_EPS = 64e-5

LANES = 128
ROUTER_LANES = LANES
RWKV_CHUNK = 64
VMEM_LIMIT = 56 * 1024 * 1024


def _ln(x, g, b, eps):
    mu = jnp.mean(x, axis=-1, keepdims=True)
    xc = x - mu
    var = jnp.mean(xc * xc, axis=-1, keepdims=True)
    return xc * lax.rsqrt(var + eps) * g + b


def _dot(a, b):
    return jnp.dot(a, b, preferred_element_type=F32)


def _split(x):
    hi = x.astype(BF16)
    lo = (x - hi.astype(F32)).astype(BF16)
    return hi, lo


def _dg(a, b, ca, cb):
    return lax.dot_general(a, b, (((ca,), (cb,)), ((), ())), preferred_element_type=F32)


def _mm(a, b, ca=1, cb=0, passes=1):
    if passes == 1:
        return _dg(a.astype(BF16), b.astype(BF16), ca, cb)
    ah, al = _split(a)
    bh, bl = _split(b)
    return _dg(ah, bh, ca, cb) + (_dg(ah, bl, ca, cb) + _dg(al, bh, ca, cb))


def _gelu_tanh(x):
    return 0.5 * x * (1.0 + jnp.tanh(0.7978845608028654 * (x + 0.044715 * (x * x * x))))


def _sigmoid(x):
    return 1.0 / (1.0 + jnp.exp(-x))


def _softplus(x):
    return jnp.maximum(x, 0.0) + jnp.log(1.0 + jnp.exp(-jnp.abs(x)))


def _const_spec(shape):
    nd = len(shape)
    return pl.BlockSpec(shape, lambda *_: (0,) * nd, pipeline_mode=pl.Buffered(1))


def _ln_kernel(x_ref, g_ref, b_ref, o_ref):
    o_ref[...] = _ln(x_ref[...], g_ref[...], b_ref[...], LN_EPS)


def _layer_norm_rows(x, g, b, tm):
    m, d = x.shape
    return pl.pallas_call(
        _ln_kernel,
        grid=(m // tm,),
        in_specs=[pl.BlockSpec((tm, d), lambda i: (i, 0)), _const_spec((1, d)), _const_spec((1, d))],
        out_specs=pl.BlockSpec((tm, d), lambda i: (i, 0)),
        out_shape=jax.ShapeDtypeStruct((m, d), F32),
        compiler_params=pltpu.CompilerParams(dimension_semantics=("parallel",)),
        name="ln_in",
    )(x, g.reshape(1, d), b.reshape(1, d))


def _proj_kernel(*refs, tm, tiles_per_seq, single_token):
    if single_token:
        (x_ref, prev_ref, win_ref, bgate_ref, lnvg_ref, lnvb_ref, wmix_ref, bmix_ref, waout_ref, mu_ref,
         w0_ref, ww2_ref, a0_ref, wa2_ref, wg2_ref, kkw_ref, kaw_ref, hsum_ref,
         ga_ref, gb_ref, va_ref, last_ref, r_ref, lw_ref, k_ref, v_ref, kn_ref, ic_ref, g_ref,
         mix_scr) = refs
    else:
        (x_ref, win_ref, bgate_ref, lnvg_ref, lnvb_ref, wmix_ref, bmix_ref, waout_ref, mu_ref,
         w0_ref, ww2_ref, a0_ref, wa2_ref, wg2_ref, kkw_ref, kaw_ref, hsum_ref,
         ga_ref, gb_ref, va_ref, last_ref, r_ref, lw_ref, k_ref, v_ref, kn_ref, ic_ref, g_ref,
         mix_scr, carry_ref) = refs

    xb = x_ref[...].astype(BF16)

    z = _gelu_tanh(_dot(xb, win_ref[:, 0:C_A]))
    u = z[:, :D_A]
    va = _ln(z[:, D_A:], lnvg_ref[...], lnvb_ref[...], LN_EPS)
    va_ref[...] = va
    vab = va.astype(BF16)
    rr = lax.broadcasted_iota(jnp.int32, (CHUNK, CHUNK), 0)
    cc = lax.broadcasted_iota(jnp.int32, (CHUNK, CHUNK), 1)
    for g in range(G_A):
        wg = jnp.where(rr >= cc, wmix_ref[g], 0.0).astype(BF16)
        for c in range(tm // CHUNK):
            rows = slice(c * CHUNK, (c + 1) * CHUNK)
            cols = slice(g * LANES, (g + 1) * LANES)
            mix_scr[rows, cols] = _dot(wg, vab[rows, cols]) + bmix_ref[:, cols]
    ya = u * mix_scr[...]
    ao = _dot(ya.astype(BF16), waout_ref[...])

    gates = _sigmoid(_dot(xb, win_ref[:, C_A + C_RWKV:]) + bgate_ref[...])
    ga_ref[...] = gates[:, :D_MODEL] * ao
    gb_ref[...] = gates[:, D_MODEL:]

    pb = _dot(xb, win_ref[:, C_A:C_A + C_RWKV])
    if single_token:
        prev = prev_ref[...]
        last_ref[...] = pb
    else:
        i = pl.program_id(0)

        @pl.when(i == 0)
        def _():
            carry_ref[...] = jnp.zeros_like(carry_ref)

        first = jnp.where(i % tiles_per_seq == 0, jnp.zeros((1, C_RWKV), F32), carry_ref[0:1, :])
        rolled = pltpu.roll(pb, 1, 0)
        row = lax.broadcasted_iota(jnp.int32, (tm, C_RWKV), 0)
        prev = jnp.where(row == 0, first, rolled)
        carry_ref[0:1, :] = pb[tm - 1:tm, :]
        last_ref[0] = pb[tm - 8:tm, :]
    xs = pb + mu_ref[...] * (prev - pb)

    r = xs[:, 0:D_B]
    k = xs[:, D_B:2 * D_B]
    v = xs[:, 2 * D_B:3 * D_B]
    o = 3 * D_B
    wa_d = xs[:, o:o + LORA_W + LORA_A]
    gd = xs[:, o + LORA_W + LORA_A:]
    w_log = -_softplus(-(w0_ref[...] + _dot(jnp.tanh(wa_d).astype(BF16), ww2_ref[...]))) - 0.5
    lw_ref[...] = -jnp.exp(w_log)
    iclr = _sigmoid(a0_ref[...] + _dot(wa_d.astype(BF16), wa2_ref[...]))
    g_ref[...] = _dot(_sigmoid(gd).astype(BF16), wg2_ref[...])
    kk = k * kkw_ref[...]
    sq_hi, sq_lo = _split(kk * kk)
    ss = _dot(sq_hi, hsum_ref[...]) + _dot(sq_lo, hsum_ref[...])
    kn_ref[...] = kk / jnp.maximum(jnp.sqrt(ss), 1e-12)
    k_ref[...] = k * (1.0 + (iclr - 1.0) * kaw_ref[...])
    r_ref[...] = r
    v_ref[...] = v
    ic_ref[...] = iclr


def _proj_call(x, prev, lw, tm, tiles_per_seq, single_token):
    m = x.shape[0]
    nt = m // tm
    row = lambda n: pl.BlockSpec((tm, n), lambda i: (i, 0))
    in_specs = [row(D_MODEL)]
    args = [x]
    if single_token:
        in_specs.append(row(C_RWKV))
        args.append(prev)
    weights = [lw["w_in"], lw["b_gate"], lw["ln_v_g"], lw["ln_v_b"], lw["wmix"], lw["bmix"], lw["w_a_out"],
               lw["mu_shift"], lw["w0"], lw["w_w2p"], lw["a0"], lw["w_a2p"], lw["w_g2"], lw["k_k"], lw["k_a"],
               lw["hsum"]]
    in_specs += [_const_spec(w.shape) for w in weights]
    args += weights
    out_shape = [jax.ShapeDtypeStruct((m, D_MODEL), F32), jax.ShapeDtypeStruct((m, D_MODEL), F32),
                 jax.ShapeDtypeStruct((m, D_A), F32)]
    out_specs = [row(D_MODEL), row(D_MODEL), row(D_A)]
    if single_token:
        out_shape.append(jax.ShapeDtypeStruct((m, C_RWKV), F32))
        out_specs.append(row(C_RWKV))
    else:
        out_shape.append(jax.ShapeDtypeStruct((nt, 8, C_RWKV), F32))
        out_specs.append(pl.BlockSpec((1, 8, C_RWKV), lambda i: (i, 0, 0)))
    out_shape += [jax.ShapeDtypeStruct((m, D_B), F32)] * 7
    out_specs += [row(D_B)] * 7
    scratch = [pltpu.VMEM((tm, D_A), F32)]
    if not single_token:
        scratch.append(pltpu.VMEM((8, C_RWKV), F32))
    return pl.pallas_call(
        functools.partial(_proj_kernel, tm=tm, tiles_per_seq=tiles_per_seq, single_token=single_token),
        grid=(nt,),
        in_specs=in_specs,
        out_specs=out_specs,
        out_shape=out_shape,
        scratch_shapes=scratch,
        compiler_params=pltpu.CompilerParams(dimension_semantics=("arbitrary",), vmem_limit_bytes=VMEM_LIMIT),
        name="proj_single" if single_token else "proj_seq",
    )(*args)


def _rwkv_out(y, r, k, v, g, lnxg, lnxb, rkw):
    yn = _ln(y, lnxg, lnxb, GN_EPS)
    bonus = jnp.sum(r * k * rkw, axis=-1, keepdims=True) * v
    return (yn + bonus) * g


def _rwkv_chunk_kernel(r_ref, lw_ref, k_ref, v_ref, kn_ref, ic_ref, g_ref, lnxg_ref, lnxb_ref, rkw_ref,
                       y_ref, sfin_ref, state_ref):
    c = pl.program_id(1)
    n_c = pl.num_programs(1)
    C = RWKV_CHUNK

    @pl.when(c == 0)
    def _():
        state_ref[...] = jnp.zeros_like(state_ref)

    row = lax.broadcasted_iota(jnp.int32, (C, C), 0)
    col = lax.broadcasted_iota(jnp.int32, (C, C), 1)
    tri_incl = jnp.where(row >= col, 1.0, 0.0).astype(BF16)
    strict = row > col
    incl = row >= col
    eye = jnp.where(row == col, 1.0, 0.0)

    lw = lw_ref[...]
    l0 = lw.astype(BF16)
    r1 = lw - l0.astype(F32)
    l1 = r1.astype(BF16)
    l2 = (r1 - l1.astype(F32)).astype(BF16)
    cum = _dot(tri_incl, l0) + (_dot(tri_incl, l1) + _dot(tri_incl, l2))
    cum_last = cum[C - 1:C, :]
    e_in = jnp.exp(cum)
    e_ex = jnp.exp(cum - lw)
    e_inv = jnp.exp(-cum)
    e_hat = jnp.exp(cum_last - cum)
    pc = jnp.exp(cum_last)

    kn = kn_ref[...]
    b = kn * ic_ref[...]
    k = k_ref[...]
    r = r_ref[...]
    v = v_ref[...]
    g = g_ref[...]
    at = -kn * e_ex
    bt = b * e_inv
    kt = k * e_inv
    rt = r * e_in
    bh = b * e_hat
    kh = k * e_hat

    for h in range(H_B):
        sl = slice(h * HEAD_B, (h + 1) * HEAD_B)
        at_h, bt_h, kt_h, rt_h, v_h = at[:, sl], bt[:, sl], kt[:, sl], rt[:, sl], v[:, sl]
        a_ab = jnp.where(strict, _mm(at_h, bt_h, 1, 1, passes=3), 0.0)
        a_ak = jnp.where(strict, _mm(at_h, kt_h, 1, 1, passes=3), 0.0)
        a_rb = jnp.where(incl, _mm(rt_h, bt_h, 1, 1), 0.0)
        a_rk = jnp.where(incl, _mm(rt_h, kt_h, 1, 1), 0.0)
        x = a_ab
        t = eye + x
        for _ in range(5):
            x = _mm(x, x, passes=3)
            t = t + _mm(t, x, passes=3)
        w2 = _mm(t, _mm(a_ak, v_h, passes=3), passes=3)
        a_t = _mm(t, at_h, passes=3)
        s = state_ref[h]
        u = _mm(a_t, s, 1, 1, passes=3) + w2
        y = _mm(rt_h, s, 1, 1) + _mm(a_rb, u) + _mm(a_rk, v_h)
        state_ref[h] = s * pc[:, sl] + _mm(u, bh[:, sl], 0, 0, passes=3) + _mm(v_h, kh[:, sl], 0, 0, passes=3)
        y_ref[:, sl] = _rwkv_out(y, r[:, sl], k[:, sl], v_h, g[:, sl], lnxg_ref[:, sl], lnxb_ref[:, sl],
                                 rkw_ref[:, sl])

    @pl.when(c == n_c - 1)
    def _():
        sfin_ref[0] = state_ref[...]


def _rwkv_seq_call(ops, lw, batch, seq):
    C = RWKV_CHUNK
    n_c = seq // C
    blk = pl.BlockSpec((C, D_B), lambda b, c: (b * n_c + c, 0))
    vec = pl.BlockSpec((1, D_B), lambda b, c: (0, 0))
    return pl.pallas_call(
        _rwkv_chunk_kernel,
        grid=(batch, n_c),
        in_specs=[blk] * 7 + [vec] * 3,
        out_specs=[blk, pl.BlockSpec((1, H_B, HEAD_B, HEAD_B), lambda b, c: (b, 0, 0, 0))],
        out_shape=[jax.ShapeDtypeStruct((batch * seq, D_B), F32),
                   jax.ShapeDtypeStruct((batch, H_B, HEAD_B, HEAD_B), F32)],
        scratch_shapes=[pltpu.VMEM((H_B, HEAD_B, HEAD_B), F32)],
        compiler_params=pltpu.CompilerParams(dimension_semantics=("arbitrary", "arbitrary")),
        name="rwkv_seq",
    )(*ops, lw["lnx_g"], lw["lnx_b"], lw["r_k"])


def _rwkv_step_kernel(s_ref, r_ref, lw_ref, k_ref, v_ref, kn_ref, ic_ref, g_ref, lnxg_ref, lnxb_ref, rkw_ref,
                      y_ref, so_ref):
    n = HEAD_B
    eye = jnp.where(lax.broadcasted_iota(jnp.int32, (n, n), 0) == lax.broadcasted_iota(jnp.int32, (n, n), 1),
                    1.0, 0.0)
    s = s_ref[...]
    r = r_ref[...]
    k = k_ref[...]
    v = v_ref[...]
    kn = kn_ref[...]
    w = jnp.exp(lw_ref[...])
    sa = jnp.sum(s * (-kn), axis=-1, keepdims=True)
    v_col = jnp.sum(eye * v, axis=-1, keepdims=True)
    s1 = s * w + sa * (kn * ic_ref[...]) + v_col * k
    so_ref[...] = s1
    y_col = jnp.sum(s1 * r, axis=-1, keepdims=True)
    y = jnp.sum(eye * y_col, axis=-2, keepdims=True)
    y_ref[...] = _rwkv_out(y, r, k, v, g_ref[...], lnxg_ref[...], lnxb_ref[...], rkw_ref[...])


def _rwkv_step_call(state, ops, lw, bb):
    nb = state.shape[0]
    hv = lambda z: z.reshape(-1, H_B, 1, HEAD_B)
    sblk = pl.BlockSpec((bb, H_B, HEAD_B, HEAD_B), lambda i: (i, 0, 0, 0))
    vblk = pl.BlockSpec((bb, H_B, 1, HEAD_B), lambda i: (i, 0, 0, 0))
    pblk = pl.BlockSpec((1, H_B, 1, HEAD_B), lambda i: (0, 0, 0, 0))
    y, s1 = pl.pallas_call(
        _rwkv_step_kernel,
        grid=(nb // bb,),
        in_specs=[sblk] + [vblk] * 7 + [pblk] * 3,
        out_specs=[vblk, sblk],
        out_shape=[jax.ShapeDtypeStruct((nb, H_B, 1, HEAD_B), F32),
                   jax.ShapeDtypeStruct((nb, H_B, HEAD_B, HEAD_B), F32)],
        compiler_params=pltpu.CompilerParams(dimension_semantics=("parallel",)),
        name="rwkv_step",
    )(state, *[hv(z) for z in ops], hv(lw["lnx_g"]), hv(lw["lnx_b"]), hv(lw["r_k"]))
    return y.reshape(nb, D_B), s1


def _route(lg, tm):
    lane = lax.broadcasted_iota(jnp.int32, (tm, ROUTER_LANES), 1)
    lanef = lane.astype(F32)
    big = jnp.float32(1e9)
    ninf = jnp.float32(-jnp.inf)
    isg = lane < N_GROUPS
    lgm = jnp.where(isg, lg, ninf)
    gmax = jnp.max(lgm, axis=-1, keepdims=True)
    den = jnp.sum(jnp.exp(lgm - gmax), axis=-1, keepdims=True)
    gp = 1.0 / den
    gi = jnp.min(jnp.where(lgm == gmax, lanef, big), axis=-1, keepdims=True)
    grp = ((lane - N_GROUPS) >> 2).astype(F32)
    ise = jnp.where(lane >= N_GROUPS, grp, big) == gi
    le = jnp.where(ise, lg, ninf)
    m1 = jnp.max(le, axis=-1, keepdims=True)
    i1 = jnp.min(jnp.where(le == m1, lanef, big), axis=-1, keepdims=True)
    le2 = jnp.where(lanef == i1, ninf, le)
    m2 = jnp.max(le2, axis=-1, keepdims=True)
    i2 = jnp.min(jnp.where(le2 == m2, lanef, big), axis=-1, keepdims=True)
    e2 = jnp.exp(m2 - m1)
    w1 = 1.0 / (1.0 + e2)
    w2 = e2 / (1.0 + e2)
    return jnp.where(lanef == i1, w1 * gp, 0.0) + jnp.where(lanef == i2, w2 * gp, 0.0)


def _mix_kernel(x_ref, ga_ref, gb_ref, yb_ref, pe_ref, wbout_ref, wo_ref, ln1g_ref, ln1b_ref, wr_ref, br_ref,
                weg_ref, weu_ref, wed_ref, wpg_ref, wpp_ref, ln2g_ref, ln2b_ref, o_ref, *, tm, dn_alpha):
    x = x_ref[...]
    bo = _dot(yb_ref[...].astype(BF16), wbout_ref[...])
    merged = ga_ref[...] + gb_ref[...] * bo
    mix = _dot(merged.astype(BF16), wo_ref[...])
    x1 = _ln(dn_alpha * x + mix, ln1g_ref[...], ln1b_ref[...], LN_EPS)
    x1b = x1.astype(BF16)

    x1h, x1l = _split(x1)
    wrh = wr_ref[0]
    wrl = wr_ref[1]
    lg = _dot(x1h, wrh) + (_dot(x1h, wrl) + _dot(x1l, wrh)) + br_ref[...]
    comb = _route(lg, tm)

    moe = jnp.zeros((tm, D_MODEL), F32)
    for grp in range(N_GROUPS):
        hs = []
        for j in range(EXPERTS_PER_GROUP):
            e = grp * EXPERTS_PER_GROUP + j
            hg = _dot(x1b, weg_ref[e])
            hu = _dot(x1b, weu_ref[e])
            ce = comb[:, N_GROUPS + e:N_GROUPS + e + 1]
            hs.append(((hg * _sigmoid(hg)) * hu * ce).astype(BF16))
        hcat = jnp.concatenate(hs, axis=1)
        rows = slice(grp * EXPERTS_PER_GROUP * D_EXPERT, (grp + 1) * EXPERTS_PER_GROUP * D_EXPERT)
        moe = moe + _dot(hcat, wed_ref[rows, :])

    ple = _sigmoid(_dot(x1b, wpg_ref[...])) * _dot(pe_ref[...].astype(BF16), wpp_ref[...])
    o_ref[...] = _ln(dn_alpha * x1 + moe + ple, ln2g_ref[...], ln2b_ref[...], LN_EPS)


def _mix_call(x, ga, gb, yb, pe, lw, tm, dn_alpha):
    m = x.shape[0]
    row = lambda n: pl.BlockSpec((tm, n), lambda i: (i, 0))
    weights = [lw["w_b_out"], lw["w_o"], lw["ln1_g"], lw["ln1_b"], lw["w_router"], lw["b_router"],
               lw["w_e_gate"], lw["w_e_up"], lw["w_e_down"], lw["w_pe_gate"], lw["w_pe_proj"],
               lw["ln2_g"], lw["ln2_b"]]
    return pl.pallas_call(
        functools.partial(_mix_kernel, tm=tm, dn_alpha=dn_alpha),
        grid=(m // tm,),
        in_specs=[row(D_MODEL), row(D_MODEL), row(D_MODEL), row(D_B), row(PLE_DIM)]
                 + [_const_spec(w.shape) for w in weights],
        out_specs=row(D_MODEL),
        out_shape=jax.ShapeDtypeStruct((m, D_MODEL), F32),
        compiler_params=pltpu.CompilerParams(dimension_semantics=("parallel",), vmem_limit_bytes=VMEM_LIMIT),
        name="mix_moe",
    )(x, ga, gb, yb, pe, *weights)


def _prep_layer(i, p):
    vec = lambda z: z.reshape(1, -1)
    bf = lambda z: z.astype(BF16)
    zeros64 = jnp.zeros((LORA_W, D_B), F32)
    w_router = jnp.concatenate(
        [p["w_rg"][i], jnp.transpose(p["w_re"][i], (1, 0, 2)).reshape(D_MODEL, N_EXPERTS),
         jnp.zeros((D_MODEL, ROUTER_LANES - N_GROUPS - N_EXPERTS), F32)], axis=1)
    wr_hi = w_router.astype(BF16)
    wr_lo = (w_router - wr_hi.astype(F32)).astype(BF16)
    b_router = jnp.concatenate([p["b_rg"][i], p["b_re"][i].reshape(-1),
                                jnp.zeros((ROUTER_LANES - N_GROUPS - N_EXPERTS,), F32)])
    head = jnp.arange(D_B) // HEAD_B
    return {
        "w_in": bf(p["w_in"][i]), "b_gate": vec(p["b_gate"][i]),
        "ln_v_g": vec(p["ln_v_g"][i]), "ln_v_b": vec(p["ln_v_b"][i]),
        "w_a_out": bf(p["w_a_out"][i]), "mu_shift": vec(p["mu_shift"][i]),
        "w0": vec(p["w0"][i]), "a0": vec(p["a0"][i]),
        "w_w2p": bf(jnp.concatenate([p["w_w2"][i], zeros64], axis=0)),
        "w_a2p": bf(jnp.concatenate([zeros64, p["w_a2"][i]], axis=0)),
        "w_g2": bf(p["w_g2"][i]), "k_k": vec(p["k_k"][i]), "k_a": vec(p["k_a"][i]),
        "hsum": (head[:, None] == head[None, :]).astype(BF16),
        "lnx_g": vec(p["lnx_g"][i]), "lnx_b": vec(p["lnx_b"][i]), "r_k": vec(p["r_k"][i]),
        "w_b_out": bf(p["w_b_out"][i]), "w_o": bf(p["w_o"][i]),
        "ln1_g": vec(p["ln1_g"][i]), "ln1_b": vec(p["ln1_b"][i]),
        "w_router": jnp.stack([wr_hi, wr_lo]), "b_router": vec(b_router),
        "w_e_gate": bf(p["w_e_gate"][i]), "w_e_up": bf(p["w_e_up"][i]),
        "w_e_down": bf(p["w_e_down"][i]).reshape(N_EXPERTS * D_EXPERT, D_MODEL),
        "w_pe_gate": bf(p["w_pe_gate"][i]), "w_pe_proj": bf(p["w_pe_proj"][i]),
        "ln2_g": vec(p["ln2_g"][i]), "ln2_b": vec(p["ln2_b"][i]),
    }


def kernel(x_prompt, x_sample, state_rwkv, state_shift, p_prompt, p_sample, ln_in_g, ln_in_b, w_in, b_gate,
           ln_v_g, ln_v_b, w_s, b_s, w_a_out, mu_shift, w0, w_w2, a0, w_a2, w_g2, k_k, k_a, r_k, lnx_g, lnx_b,
           w_b_out, w_o, ln1_g, ln1_b, w_rg, b_rg, w_re, b_re, w_e_gate, w_e_up, w_e_down, w_pe_gate,
           w_pe_proj, ln2_g, ln2_b):
    p = dict(w_in=w_in, b_gate=b_gate, ln_v_g=ln_v_g, ln_v_b=ln_v_b, w_a_out=w_a_out, mu_shift=mu_shift,
             w0=w0, w_w2=w_w2, a0=a0, w_a2=w_a2, w_g2=w_g2, k_k=k_k, k_a=k_a, r_k=r_k, lnx_g=lnx_g,
             lnx_b=lnx_b, w_b_out=w_b_out, w_o=w_o, ln1_g=ln1_g, ln1_b=ln1_b, w_rg=w_rg, b_rg=b_rg,
             w_re=w_re, b_re=b_re, w_e_gate=w_e_gate, w_e_up=w_e_up, w_e_down=w_e_down,
             w_pe_gate=w_pe_gate, w_pe_proj=w_pe_proj, ln2_g=ln2_g, ln2_b=ln2_b)
    batch, seq, _ = x_prompt.shape
    nb, dec_seq, _ = x_sample.shape
    depth = w_in.shape[0]
    assert dec_seq == 1 and seq % CHUNK == 0 and seq % RWKV_CHUNK == 0
    dn_alpha = (2 * depth) ** 0.25
    tm = 256
    tms = nb
    tiles_per_seq = seq // tm
    last_chunk = ((seq - 1) // CHUNK) * CHUNK

    xp = _layer_norm_rows(x_prompt.reshape(batch * seq, D_MODEL), ln_in_g, ln_in_b, tm)
    xs = _layer_norm_rows(x_sample.reshape(nb, D_MODEL), ln_in_g, ln_in_b, tms)

    eye = jnp.eye(CHUNK, dtype=F32)
    outs = [[] for _ in range(6)]
    for i in range(depth):
        lw = _prep_layer(i, p)
        lw_p = dict(lw, wmix=w_s[i], bmix=jnp.repeat(jnp.transpose(b_s[i]), LANES, axis=1))
        lw_s = dict(lw, wmix=w_s[i][:, 0, 0][:, None, None] * eye[None],
                    bmix=jnp.broadcast_to(jnp.repeat(b_s[i][:, 0], LANES)[None, :], (CHUNK, D_A)))

        res = _proj_call(xp, None, lw_p, tm, tiles_per_seq, False)
        ga, gb, va, last = res[:4]
        yb, s_fin = _rwkv_seq_call(res[4:], lw, batch, seq)
        xp = _mix_call(xp, ga, gb, yb, p_prompt[i].reshape(batch * seq, PLE_DIM), lw, tm, dn_alpha)
        outs[0].append(s_fin)
        outs[1].append(last[tiles_per_seq - 1::tiles_per_seq, 7, :])
        outs[2].append(va.reshape(batch, seq, D_A)[:, last_chunk:])

        res = _proj_call(xs, state_shift[i], lw_s, tms, 1, True)
        ga, gb, va, pb = res[:4]
        yb, s1 = _rwkv_step_call(state_rwkv[i], res[4:], lw, 8)
        xs = _mix_call(xs, ga, gb, yb, p_sample[i].reshape(nb, PLE_DIM), lw, tms, dn_alpha)
        outs[3].append(s1)
        outs[4].append(pb)
        outs[5].append(va.reshape(nb, 1, D_A))

    return (xp.reshape(batch, seq, D_MODEL), xs.reshape(nb, 1, D_MODEL),
            jnp.stack(outs[0]), jnp.stack(outs[1]), jnp.stack(outs[2]),
            jnp.stack(outs[3]), jnp.stack(outs[4]), jnp.stack(outs[5]))
```

```python
import functools

import jax
import jax.numpy as jnp
from jax import lax
from jax.experimental import pallas as pl
from jax.experimental.pallas import tpu as pltpu

F32 = jnp.float32
BF16 = jnp.bfloat16

D_MODEL = 1024
CHUNK = 128
D_A = 512
G_A = 4
D_B = 512
HEAD_B = 64
H_B = D_B // HEAD_B
LORA_W = 64
LORA_A = 64
LORA_G = 128
C_A = 2 * D_A
C_RWKV = 3 * D_B + LORA_W + LORA_A + LORA_G
C_GATE = 2 * D_MODEL
C_IN = C_A + C_RWKV + C_GATE
N_GROUPS = 4
EXPERTS_PER_GROUP = 4
N_EXPERTS = N_GROUPS * EXPERTS_PER_GROUP
D_EXPERT = 256
PLE_DIM = 256
LN_EPS = 1e-5
GN_EPS = 64e-5

LANES = 128
ROUTER_LANES = LANES
RWKV_CHUNK = 64
RWKV_TILE = 256
VMEM_LIMIT = 56 * 1024 * 1024


def _ln(x, g, b, eps):
    mu = jnp.mean(x, axis=-1, keepdims=True)
    xc = x - mu
    var = jnp.mean(xc * xc, axis=-1, keepdims=True)
    return xc * lax.rsqrt(var + eps) * g + b


def _dot(a, b):
    return jnp.dot(a, b, preferred_element_type=F32)


def _split(x):
    hi = x.astype(BF16)
    lo = (x - hi.astype(F32)).astype(BF16)
    return hi, lo


def _dg(a, b, ca, cb):
    return lax.dot_general(a, b, (((ca,), (cb,)), ((), ())), preferred_element_type=F32)


def _mm(a, b, ca=1, cb=0, passes=1):
    if passes == 1:
        return _dg(a.astype(BF16), b.astype(BF16), ca, cb)
    ah, al = _split(a)
    bh, bl = _split(b)
    return _dg(ah, bh, ca, cb) + (_dg(ah, bl, ca, cb) + _dg(al, bh, ca, cb))


def _gelu_tanh(x):
    return 0.5 * x * (1.0 + jnp.tanh(0.7978845608028654 * (x + 0.044715 * (x * x * x))))


def _sigmoid(x):
    return 1.0 / (1.0 + jnp.exp(-x))


def _softplus(x):
    return jnp.maximum(x, 0.0) + jnp.log(1.0 + jnp.exp(-jnp.abs(x)))


def _const_spec(shape):
    nd = len(shape)
    return pl.BlockSpec(shape, lambda *_: (0,) * nd, pipeline_mode=pl.Buffered(1))


def _ln_kernel(x_ref, g_ref, b_ref, o_ref):
    o_ref[...] = _ln(x_ref[...], g_ref[...], b_ref[...], LN_EPS)


def _layer_norm_rows(x, g, b, tm):
    m, d = x.shape
    return pl.pallas_call(
        _ln_kernel,
        grid=(m // tm,),
        in_specs=[pl.BlockSpec((tm, d), lambda i: (i, 0)), _const_spec((1, d)), _const_spec((1, d))],
        out_specs=pl.BlockSpec((tm, d), lambda i: (i, 0)),
        out_shape=jax.ShapeDtypeStruct((m, d), F32),
        compiler_params=pltpu.CompilerParams(dimension_semantics=("parallel",)),
        name="ln_in",
    )(x, g.reshape(1, d), b.reshape(1, d))


def _proj_kernel(*refs, tm, tiles_per_seq, single_token):
    if single_token:
        (x_ref, prev_ref, win_ref, bgate_ref, lnvg_ref, lnvb_ref, wmix_ref, bmix_ref, waout_ref, mu_ref,
         w0_ref, ww2_ref, a0_ref, wa2_ref, wg2_ref, kkw_ref, kaw_ref, hsum_ref,
         ga_ref, gb_ref, va_ref, last_ref, r_ref, lw_ref, k_ref, v_ref, kn_ref, ic_ref, g_ref,
         mix_scr) = refs
    else:
        (x_ref, win_ref, bgate_ref, lnvg_ref, lnvb_ref, wmix_ref, bmix_ref, waout_ref, mu_ref,
         w0_ref, ww2_ref, a0_ref, wa2_ref, wg2_ref, kkw_ref, kaw_ref, hsum_ref,
         ga_ref, gb_ref, va_ref, last_ref, r_ref, lw_ref, k_ref, v_ref, kn_ref, ic_ref, g_ref,
         mix_scr, carry_ref) = refs

    xb = x_ref[...].astype(BF16)

    z = _gelu_tanh(_dot(xb, win_ref[:, 0:C_A]))
    u = z[:, :D_A]
    va = _ln(z[:, D_A:], lnvg_ref[...], lnvb_ref[...], LN_EPS)
    va_ref[...] = va
    vab = va.astype(BF16)
    rr = lax.broadcasted_iota(jnp.int32, (CHUNK, CHUNK), 0)
    cc = lax.broadcasted_iota(jnp.int32, (CHUNK, CHUNK), 1)
    for g in range(G_A):
        wg = jnp.where(rr >= cc, wmix_ref[g], 0.0).astype(BF16)
        for c in range(tm // CHUNK):
            rows = slice(c * CHUNK, (c + 1) * CHUNK)
            cols = slice(g * LANES, (g + 1) * LANES)
            mix_scr[rows, cols] = _dot(wg, vab[rows, cols]) + bmix_ref[:, cols]
    ya = u * mix_scr[...]
    ao = _dot(ya.astype(BF16), waout_ref[...])

    gates = _sigmoid(_dot(xb, win_ref[:, C_A + C_RWKV:]) + bgate_ref[...])
    ga_ref[...] = gates[:, :D_MODEL] * ao
    gb_ref[...] = gates[:, D_MODEL:]

    pb = _dot(xb, win_ref[:, C_A:C_A + C_RWKV])
    if single_token:
        prev = prev_ref[...]
        last_ref[...] = pb
    else:
        i = pl.program_id(0)

        @pl.when(i == 0)
        def _():
            carry_ref[...] = jnp.zeros_like(carry_ref)

        first = jnp.where(i % tiles_per_seq == 0, jnp.zeros((1, C_RWKV), F32), carry_ref[0:1, :])
        rolled = pltpu.roll(pb, 1, 0)
        row = lax.broadcasted_iota(jnp.int32, (tm, C_RWKV), 0)
        prev = jnp.where(row == 0, first, rolled)
        carry_ref[0:1, :] = pb[tm - 1:tm, :]
        last_ref[0] = pb[tm - 8:tm, :]
    xs = pb + mu_ref[...] * (prev - pb)

    r = xs[:, 0:D_B]
    k = xs[:, D_B:2 * D_B]
    v = xs[:, 2 * D_B:3 * D_B]
    o = 3 * D_B
    wa_d = xs[:, o:o + LORA_W + LORA_A]
    gd = xs[:, o + LORA_W + LORA_A:]
    w_log = -_softplus(-(w0_ref[...] + _dot(jnp.tanh(wa_d).astype(BF16), ww2_ref[...]))) - 0.5
    lw_ref[...] = -jnp.exp(w_log)
    iclr = _sigmoid(a0_ref[...] + _dot(wa_d.astype(BF16), wa2_ref[...]))
    g_ref[...] = _dot(_sigmoid(gd).astype(BF16), wg2_ref[...])
    kk = k * kkw_ref[...]
    sq_hi, sq_lo = _split(kk * kk)
    ss = _dot(sq_hi, hsum_ref[...]) + _dot(sq_lo, hsum_ref[...])
    kn_ref[...] = kk / jnp.maximum(jnp.sqrt(ss), 1e-12)
    k_ref[...] = k * (1.0 + (iclr - 1.0) * kaw_ref[...])
    r_ref[...] = r
    v_ref[...] = v
    ic_ref[...] = iclr


def _proj_call(x, prev, lw, tm, tiles_per_seq, single_token):
    m = x.shape[0]
    nt = m // tm
    row = lambda n: pl.BlockSpec((tm, n), lambda i: (i, 0))
    in_specs = [row(D_MODEL)]
    args = [x]
    if single_token:
        in_specs.append(row(C_RWKV))
        args.append(prev)
    weights = [lw["w_in"], lw["b_gate"], lw["ln_v_g"], lw["ln_v_b"], lw["wmix"], lw["bmix"], lw["w_a_out"],
               lw["mu_shift"], lw["w0"], lw["w_w2p"], lw["a0"], lw["w_a2p"], lw["w_g2"], lw["k_k"], lw["k_a"],
               lw["hsum"]]
    in_specs += [_const_spec(w.shape) for w in weights]
    args += weights
    out_shape = [jax.ShapeDtypeStruct((m, D_MODEL), F32), jax.ShapeDtypeStruct((m, D_MODEL), F32),
                 jax.ShapeDtypeStruct((m, D_A), F32)]
    out_specs = [row(D_MODEL), row(D_MODEL), row(D_A)]
    if single_token:
        out_shape.append(jax.ShapeDtypeStruct((m, C_RWKV), F32))
        out_specs.append(row(C_RWKV))
    else:
        out_shape.append(jax.ShapeDtypeStruct((nt, 8, C_RWKV), F32))
        out_specs.append(pl.BlockSpec((1, 8, C_RWKV), lambda i: (i, 0, 0)))
    out_shape += [jax.ShapeDtypeStruct((m, D_B), F32)] * 7
    out_specs += [row(D_B)] * 7
    scratch = [pltpu.VMEM((tm, D_A), F32)]
    if not single_token:
        scratch.append(pltpu.VMEM((8, C_RWKV), F32))
    return pl.pallas_call(
        functools.partial(_proj_kernel, tm=tm, tiles_per_seq=tiles_per_seq, single_token=single_token),
        grid=(nt,),
        in_specs=in_specs,
        out_specs=out_specs,
        out_shape=out_shape,
        scratch_shapes=scratch,
        compiler_params=pltpu.CompilerParams(dimension_semantics=("arbitrary",), vmem_limit_bytes=VMEM_LIMIT),
        name="proj_single" if single_token else "proj_seq",
    )(*args)


def _rwkv_out(y, r, k, v, g, lnxg, lnxb, rkw):
    yn = _ln(y, lnxg, lnxb, GN_EPS)
    bonus = jnp.sum(r * k * rkw, axis=-1, keepdims=True) * v
    return (yn + bonus) * g


def _rwkv_tile_kernel(r_ref, lw_ref, k_ref, v_ref, kn_ref, ic_ref, g_ref, lnxg_ref, lnxb_ref, rkw_ref, hh_ref,
                      y_ref, sfin_ref, state_ref):
    ti = pl.program_id(1)
    n_t = pl.num_programs(1)
    C = RWKV_CHUNK
    RT = RWKV_TILE
    n_ch = RT // C

    @pl.when(ti == 0)
    def _():
        state_ref[...] = jnp.zeros_like(state_ref)

    row = lax.broadcasted_iota(jnp.int32, (RT, RT), 0)
    col = lax.broadcasted_iota(jnp.int32, (RT, RT), 1)
    same = (row // C) == (col // C)
    incl = jnp.logical_and(same, row >= col)
    strict = jnp.logical_and(same, row > col)
    tri = jnp.where(incl, 1.0, 0.0).astype(BF16)
    eye = jnp.where(row == col, 1.0, 0.0)
    prow = lax.broadcasted_iota(jnp.int32, (LANES, LANES), 0)
    pcol = lax.broadcasted_iota(jnp.int32, (LANES, LANES), 1)
    pair_bd = (prow // HEAD_B) == (pcol // HEAD_B)
    lane = lax.broadcasted_iota(jnp.int32, (1, D_B), 1)
    head_even = jnp.where((lane // HEAD_B) % 2 == 0, 1.0, 0.0)
    head_odd = 1.0 - head_even
    first_head = lax.broadcasted_iota(jnp.int32, (1, LANES), 1) < HEAD_B

    lw = lw_ref[...]
    l0 = lw.astype(BF16)
    r1 = lw - l0.astype(F32)
    l1 = r1.astype(BF16)
    l2 = (r1 - l1.astype(F32)).astype(BF16)
    cum = _dot(tri, l0) + (_dot(tri, l1) + _dot(tri, l2))
    ends = [cum[(c + 1) * C - 1:(c + 1) * C, :] for c in range(n_ch)]
    cum_end = jnp.concatenate([jnp.broadcast_to(e, (C, D_B)) for e in ends], axis=0)
    e_in = jnp.exp(cum)
    e_ex = jnp.exp(cum - lw)
    e_inv = jnp.exp(-cum)
    e_hat = jnp.exp(cum_end - cum)

    kn = kn_ref[...]
    b = kn * ic_ref[...]
    k = k_ref[...]
    r = r_ref[...]
    v = v_ref[...]
    at = -kn * e_ex
    rt = r * e_in
    at_m = ((at * head_even).astype(BF16), (at * head_odd).astype(BF16))
    rt_m = ((rt * head_even).astype(BF16), (rt * head_odd).astype(BF16))
    bt_b = (b * e_inv).astype(BF16)
    kt_b = (k * e_inv).astype(BF16)
    bh_b = (b * e_hat).astype(BF16)
    kh_b = (k * e_hat).astype(BF16)
    v_b = v.astype(BF16)
    hh2 = jnp.concatenate([hh_ref[...], hh_ref[...]], axis=0)

    def head_sum(z):
        zh, zl = _split(z)
        return _dot(jnp.concatenate([zh, zl], axis=1), hh2)

    n_pair = H_B // 2
    lanes_of = [slice((h // 2) * LANES, (h // 2 + 1) * LANES) for h in range(H_B)]
    atm = [at_m[h % 2][:, lanes_of[h]] for h in range(H_B)]
    rtm = [rt_m[h % 2][:, lanes_of[h]] for h in range(H_B)]

    xs = [jnp.where(strict, _dg(atm[h], bt_b[:, lanes_of[h]], 1, 1), 0.0) for h in range(H_B)]
    ts = [eye + x for x in xs]
    xb = [x.astype(BF16) for x in xs]
    for _ in range(5):
        xb = [_dot(z, z).astype(BF16) for z in xb]
        ts = [t + _dot(t.astype(BF16), z) for t, z in zip(ts, xb)]
    tb = [t.astype(BF16) for t in ts]
    a_ak = [jnp.where(strict, _dg(atm[h], kt_b[:, lanes_of[h]], 1, 1), 0.0).astype(BF16) for h in range(H_B)]
    av = [_dot(a_ak[h], v_b[:, lanes_of[h]]).astype(BF16) for h in range(H_B)]
    tx = [_dot(tb[h], jnp.concatenate([av[h], atm[h]], axis=1)) for h in range(H_B)]
    a_rb = [jnp.where(incl, _dg(rtm[h], bt_b[:, lanes_of[h]], 1, 1), 0.0).astype(BF16) for h in range(H_B)]
    a_rk = [jnp.where(incl, _dg(rtm[h], kt_b[:, lanes_of[h]], 1, 1), 0.0).astype(BF16) for h in range(H_B)]
    rb = [_dot(a_rb[h], tx[h].astype(BF16)) for h in range(H_B)]
    rkv = [_dot(a_rk[h], v_b[:, lanes_of[h]]) for h in range(H_B)]

    sls = [slice(p * LANES, (p + 1) * LANES) for p in range(n_pair)]
    w2 = [jnp.where(first_head, tx[2 * p][:, :LANES], tx[2 * p + 1][:, :LANES]).astype(BF16) for p in range(n_pair)]
    a_t = [(tx[2 * p][:, LANES:] + tx[2 * p + 1][:, LANES:]).astype(BF16) for p in range(n_pair)]
    rhat = [(rt[:, sls[p]] + (rb[2 * p][:, LANES:] + rb[2 * p + 1][:, LANES:])).astype(BF16) for p in range(n_pair)]
    y0 = [jnp.where(first_head, rb[2 * p][:, :LANES] + rkv[2 * p], rb[2 * p + 1][:, :LANES] + rkv[2 * p + 1])
          for p in range(n_pair)]

    s = [state_ref[p] for p in range(n_pair)]
    ys = [[] for _ in range(n_pair)]
    for c in range(n_ch):
        rows = slice(c * C, (c + 1) * C)
        el = [jnp.where(pair_bd, _dg(a_t[p][rows], bh_b[rows, sls[p]], 0, 0), 0.0).astype(BF16)
              for p in range(n_pair)]
        gm = [jnp.where(pair_bd,
                        _dg(jnp.concatenate([w2[p][rows], v_b[rows, sls[p]]], axis=0),
                            jnp.concatenate([bh_b[rows, sls[p]], kh_b[rows, sls[p]]], axis=0), 0, 0), 0.0)
              for p in range(n_pair)]
        sb = [z.astype(BF16) for z in s]
        for p in range(n_pair):
            ys[p].append(_dg(rhat[p][rows], sb[p], 1, 1) + y0[p][rows])
        s = [s[p] * jnp.exp(ends[c][:, sls[p]]) + _dot(sb[p], el[p]) + gm[p] for p in range(n_pair)]
    for p in range(n_pair):
        state_ref[p] = s[p]

    for p in range(n_pair):
        sl = sls[p]
        y = jnp.concatenate(ys[p], axis=0)
        mu = head_sum(y) * (1.0 / HEAD_B)
        yc = y - mu
        var = head_sum(yc * yc) * (1.0 / HEAD_B)
        yn = yc * lax.rsqrt(var + GN_EPS) * lnxg_ref[:, sl] + lnxb_ref[:, sl]
        bonus = head_sum(r[:, sl] * k[:, sl] * rkw_ref[:, sl]) * v[:, sl]
        y_ref[:, sl] = (yn + bonus) * g_ref[:, sl]

    @pl.when(ti == n_t - 1)
    def _():
        sfin_ref[0] = state_ref[...]


def _rwkv_seq_call(ops, lw, batch, seq):
    RT = RWKV_TILE
    n_t = seq // RT
    n_pair = H_B // 2
    blk = pl.BlockSpec((RT, D_B), lambda b, t: (b * n_t + t, 0))
    vec = pl.BlockSpec((1, D_B), lambda b, t: (0, 0))
    y, s_bd = pl.pallas_call(
        _rwkv_tile_kernel,
        grid=(batch, n_t),
        in_specs=[blk] * 7 + [vec] * 3 + [pl.BlockSpec((LANES, LANES), lambda b, t: (0, 0))],
        out_specs=[blk, pl.BlockSpec((1, n_pair, LANES, LANES), lambda b, t: (b, 0, 0, 0))],
        out_shape=[jax.ShapeDtypeStruct((batch * seq, D_B), F32),
                   jax.ShapeDtypeStruct((batch, n_pair, LANES, LANES), F32)],
        scratch_shapes=[pltpu.VMEM((n_pair, LANES, LANES), F32)],
        compiler_params=pltpu.CompilerParams(dimension_semantics=("arbitrary", "arbitrary"),
                                             vmem_limit_bytes=VMEM_LIMIT),
        name="rwkv_seq",
    )(*ops, lw["lnx_g"], lw["lnx_b"], lw["r_k"], lw["hsum"][:LANES, :LANES])
    s_bd = s_bd.reshape(batch, n_pair, 2, HEAD_B, 2, HEAD_B)
    s_fin = jnp.stack([s_bd[:, :, 0, :, 0, :], s_bd[:, :, 1, :, 1, :]], axis=2)
    return y, s_fin.reshape(batch, H_B, HEAD_B, HEAD_B)


def _rwkv_step_kernel(s_ref, r_ref, lw_ref, k_ref, v_ref, kn_ref, ic_ref, g_ref, lnxg_ref, lnxb_ref, rkw_ref,
                      y_ref, so_ref):
    n = HEAD_B
    eye = jnp.where(lax.broadcasted_iota(jnp.int32, (n, n), 0) == lax.broadcasted_iota(jnp.int32, (n, n), 1),
                    1.0, 0.0)
    s = s_ref[...]
    r = r_ref[...]
    k = k_ref[...]
    v = v_ref[...]
    kn = kn_ref[...]
    w = jnp.exp(lw_ref[...])
    sa = jnp.sum(s * (-kn), axis=-1, keepdims=True)
    v_col = jnp.sum(eye * v, axis=-1, keepdims=True)
    s1 = s * w + sa * (kn * ic_ref[...]) + v_col * k
    so_ref[...] = s1
    y_col = jnp.sum(s1 * r, axis=-1, keepdims=True)
    y = jnp.sum(eye * y_col, axis=-2, keepdims=True)
    y_ref[...] = _rwkv_out(y, r, k, v, g_ref[...], lnxg_ref[...], lnxb_ref[...], rkw_ref[...])


def _rwkv_step_call(state, ops, lw, bb):
    nb = state.shape[0]
    hv = lambda z: z.reshape(-1, H_B, 1, HEAD_B)
    sblk = pl.BlockSpec((bb, H_B, HEAD_B, HEAD_B), lambda i: (i, 0, 0, 0))
    vblk = pl.BlockSpec((bb, H_B, 1, HEAD_B), lambda i: (i, 0, 0, 0))
    pblk = pl.BlockSpec((1, H_B, 1, HEAD_B), lambda i: (0, 0, 0, 0))
    y, s1 = pl.pallas_call(
        _rwkv_step_kernel,
        grid=(nb // bb,),
        in_specs=[sblk] + [vblk] * 7 + [pblk] * 3,
        out_specs=[vblk, sblk],
        out_shape=[jax.ShapeDtypeStruct((nb, H_B, 1, HEAD_B), F32),
                   jax.ShapeDtypeStruct((nb, H_B, HEAD_B, HEAD_B), F32)],
        compiler_params=pltpu.CompilerParams(dimension_semantics=("parallel",)),
        name="rwkv_step",
    )(state, *[hv(z) for z in ops], hv(lw["lnx_g"]), hv(lw["lnx_b"]), hv(lw["r_k"]))
    return y.reshape(nb, D_B), s1


def _route(lg, tm):
    lane = lax.broadcasted_iota(jnp.int32, (tm, ROUTER_LANES), 1)
    lanef = lane.astype(F32)
    big = jnp.float32(1e9)
    ninf = jnp.float32(-jnp.inf)
    isg = lane < N_GROUPS
    lgm = jnp.where(isg, lg, ninf)
    gmax = jnp.max(lgm, axis=-1, keepdims=True)
    den = jnp.sum(jnp.exp(lgm - gmax), axis=-1, keepdims=True)
    gp = 1.0 / den
    gi = jnp.min(jnp.where(lgm == gmax, lanef, big), axis=-1, keepdims=True)
    grp = ((lane - N_GROUPS) >> 2).astype(F32)
    ise = jnp.where(lane >= N_GROUPS, grp, big) == gi
    le = jnp.where(ise, lg, ninf)
    m1 = jnp.max(le, axis=-1, keepdims=True)
    i1 = jnp.min(jnp.where(le == m1, lanef, big), axis=-1, keepdims=True)
    le2 = jnp.where(lanef == i1, ninf, le)
    m2 = jnp.max(le2, axis=-1, keepdims=True)
    i2 = jnp.min(jnp.where(le2 == m2, lanef, big), axis=-1, keepdims=True)
    e2 = jnp.exp(m2 - m1)
    w1 = 1.0 / (1.0 + e2)
    w2 = e2 / (1.0 + e2)
    return jnp.where(lanef == i1, w1 * gp, 0.0) + jnp.where(lanef == i2, w2 * gp, 0.0)


def _mix_kernel(x_ref, ga_ref, gb_ref, yb_ref, pe_ref, wbout_ref, wo_ref, ln1g_ref, ln1b_ref, wr_ref, br_ref,
                weg_ref, weu_ref, wed_ref, wpg_ref, wpp_ref, ln2g_ref, ln2b_ref, o_ref, *, tm, dn_alpha):
    x = x_ref[...]
    bo = _dot(yb_ref[...].astype(BF16), wbout_ref[...])
    merged = ga_ref[...] + gb_ref[...] * bo
    mix = _dot(merged.astype(BF16), wo_ref[...])
    x1 = _ln(dn_alpha * x + mix, ln1g_ref[...], ln1b_ref[...], LN_EPS)
    x1b = x1.astype(BF16)

    x1h, x1l = _split(x1)
    wrh = wr_ref[0]
    wrl = wr_ref[1]
    lg = _dot(x1h, wrh) + (_dot(x1h, wrl) + _dot(x1l, wrh)) + br_ref[...]
    comb = _route(lg, tm)

    moe = jnp.zeros((tm, D_MODEL), F32)
    for grp in range(N_GROUPS):
        hs = []
        for j in range(EXPERTS_PER_GROUP):
            e = grp * EXPERTS_PER_GROUP + j
            hg = _dot(x1b, weg_ref[e])
            hu = _dot(x1b, weu_ref[e])
            ce = comb[:, N_GROUPS + e:N_GROUPS + e + 1]
            hs.append(((hg * _sigmoid(hg)) * hu * ce).astype(BF16))
        hcat = jnp.concatenate(hs, axis=1)
        rows = slice(grp * EXPERTS_PER_GROUP * D_EXPERT, (grp + 1) * EXPERTS_PER_GROUP * D_EXPERT)
        moe = moe + _dot(hcat, wed_ref[rows, :])

    ple = _sigmoid(_dot(x1b, wpg_ref[...])) * _dot(pe_ref[...].astype(BF16), wpp_ref[...])
    o_ref[...] = _ln(dn_alpha * x1 + moe + ple, ln2g_ref[...], ln2b_ref[...], LN_EPS)


def _mix_call(x, ga, gb, yb, pe, lw, tm, dn_alpha):
    m = x.shape[0]
    row = lambda n: pl.BlockSpec((tm, n), lambda i: (i, 0))
    weights = [lw["w_b_out"], lw["w_o"], lw["ln1_g"], lw["ln1_b"], lw["w_router"], lw["b_router"],
               lw["w_e_gate"], lw["w_e_up"], lw["w_e_down"], lw["w_pe_gate"], lw["w_pe_proj"],
               lw["ln2_g"], lw["ln2_b"]]
    return pl.pallas_call(
        functools.partial(_mix_kernel, tm=tm, dn_alpha=dn_alpha),
        grid=(m // tm,),
        in_specs=[row(D_MODEL), row(D_MODEL), row(D_MODEL), row(D_B), row(PLE_DIM)]
                 + [_const_spec(w.shape) for w in weights],
        out_specs=row(D_MODEL),
        out_shape=jax.ShapeDtypeStruct((m, D_MODEL), F32),
        compiler_params=pltpu.CompilerParams(dimension_semantics=("parallel",), vmem_limit_bytes=VMEM_LIMIT),
        name="mix_moe",
    )(x, ga, gb, yb, pe, *weights)


def _prep_layer(i, p):
    vec = lambda z: z.reshape(1, -1)
    bf = lambda z: z.astype(BF16)
    zeros64 = jnp.zeros((LORA_W, D_B), F32)
    w_router = jnp.concatenate(
        [p["w_rg"][i], jnp.transpose(p["w_re"][i], (1, 0, 2)).reshape(D_MODEL, N_EXPERTS),
         jnp.zeros((D_MODEL, ROUTER_LANES - N_GROUPS - N_EXPERTS), F32)], axis=1)
    wr_hi = w_router.astype(BF16)
    wr_lo = (w_router - wr_hi.astype(F32)).astype(BF16)
    b_router = jnp.concatenate([p["b_rg"][i], p["b_re"][i].reshape(-1),
                                jnp.zeros((ROUTER_LANES - N_GROUPS - N_EXPERTS,), F32)])
    head = jnp.arange(D_B) // HEAD_B
    return {
        "w_in": bf(p["w_in"][i]), "b_gate": vec(p["b_gate"][i]),
        "ln_v_g": vec(p["ln_v_g"][i]), "ln_v_b": vec(p["ln_v_b"][i]),
        "w_a_out": bf(p["w_a_out"][i]), "mu_shift": vec(p["mu_shift"][i]),
        "w0": vec(p["w0"][i]), "a0": vec(p["a0"][i]),
        "w_w2p": bf(jnp.concatenate([p["w_w2"][i], zeros64], axis=0)),
        "w_a2p": bf(jnp.concatenate([zeros64, p["w_a2"][i]], axis=0)),
        "w_g2": bf(p["w_g2"][i]), "k_k": vec(p["k_k"][i]), "k_a": vec(p["k_a"][i]),
        "hsum": (head[:, None] == head[None, :]).astype(BF16),
        "lnx_g": vec(p["lnx_g"][i]), "lnx_b": vec(p["lnx_b"][i]), "r_k": vec(p["r_k"][i]),
        "w_b_out": bf(p["w_b_out"][i]), "w_o": bf(p["w_o"][i]),
        "ln1_g": vec(p["ln1_g"][i]), "ln1_b": vec(p["ln1_b"][i]),
        "w_router": jnp.stack([wr_hi, wr_lo]), "b_router": vec(b_router),
        "w_e_gate": bf(p["w_e_gate"][i]), "w_e_up": bf(p["w_e_up"][i]),
        "w_e_down": bf(p["w_e_down"][i]).reshape(N_EXPERTS * D_EXPERT, D_MODEL),
        "w_pe_gate": bf(p["w_pe_gate"][i]), "w_pe_proj": bf(p["w_pe_proj"][i]),
        "ln2_g": vec(p["ln2_g"][i]), "ln2_b": vec(p["ln2_b"][i]),
    }


def kernel(x_prompt, x_sample, state_rwkv, state_shift, p_prompt, p_sample, ln_in_g, ln_in_b, w_in, b_gate,
           ln_v_g, ln_v_b, w_s, b_s, w_a_out, mu_shift, w0, w_w2, a0, w_a2, w_g2, k_k, k_a, r_k, lnx_g, lnx_b,
           w_b_out, w_o, ln1_g, ln1_b, w_rg, b_rg, w_re, b_re, w_e_gate, w_e_up, w_e_down, w_pe_gate,
           w_pe_proj, ln2_g, ln2_b):
    p = dict(w_in=w_in, b_gate=b_gate, ln_v_g=ln_v_g, ln_v_b=ln_v_b, w_a_out=w_a_out, mu_shift=mu_shift,
             w0=w0, w_w2=w_w2, a0=a0, w_a2=w_a2, w_g2=w_g2, k_k=k_k, k_a=k_a, r_k=r_k, lnx_g=lnx_g,
             lnx_b=lnx_b, w_b_out=w_b_out, w_o=w_o, ln1_g=ln1_g, ln1_b=ln1_b, w_rg=w_rg, b_rg=b_rg,
             w_re=w_re, b_re=b_re, w_e_gate=w_e_gate, w_e_up=w_e_up, w_e_down=w_e_down,
             w_pe_gate=w_pe_gate, w_pe_proj=w_pe_proj, ln2_g=ln2_g, ln2_b=ln2_b)
    batch, seq, _ = x_prompt.shape
    nb, dec_seq, _ = x_sample.shape
    depth = w_in.shape[0]
    assert dec_seq == 1 and seq % CHUNK == 0 and seq % RWKV_TILE == 0
    dn_alpha = (2 * depth) ** 0.25
    tm = 256
    tms = nb
    tiles_per_seq = seq // tm
    last_chunk = ((seq - 1) // CHUNK) * CHUNK

    xp = _layer_norm_rows(x_prompt.reshape(batch * seq, D_MODEL), ln_in_g, ln_in_b, tm)
    xs = _layer_norm_rows(x_sample.reshape(nb, D_MODEL), ln_in_g, ln_in_b, tms)

    eye = jnp.eye(CHUNK, dtype=F32)
    outs = [[] for _ in range(6)]
    for i in range(depth):
        lw = _prep_layer(i, p)
        lw_p = dict(lw, wmix=w_s[i], bmix=jnp.repeat(jnp.transpose(b_s[i]), LANES, axis=1))
        lw_s = dict(lw, wmix=w_s[i][:, 0, 0][:, None, None] * eye[None],
                    bmix=jnp.broadcast_to(jnp.repeat(b_s[i][:, 0], LANES)[None, :], (CHUNK, D_A)))

        res = _proj_call(xp, None, lw_p, tm, tiles_per_seq, False)
        ga, gb, va, last = res[:4]
        yb, s_fin = _rwkv_seq_call(res[4:], lw, batch, seq)
        xp = _mix_call(xp, ga, gb, yb, p_prompt[i].reshape(batch * seq, PLE_DIM), lw, tm, dn_alpha)
        outs[0].append(s_fin)
        outs[1].append(last[tiles_per_seq - 1::tiles_per_seq, 7, :])
        outs[2].append(va.reshape(batch, seq, D_A)[:, last_chunk:])

        res = _proj_call(xs, state_shift[i], lw_s, tms, 1, True)
        ga, gb, va, pb = res[:4]
        yb, s1 = _rwkv_step_call(state_rwkv[i], res[4:], lw, 8)
        xs = _mix_call(xs, ga, gb, yb, p_sample[i].reshape(nb, PLE_DIM), lw, tms, dn_alpha)
        outs[3].append(s1)
        outs[4].append(pb)
        outs[5].append(va.reshape(nb, 1, D_A))

    return (xp.reshape(batch, seq, D_MODEL), xs.reshape(nb, 1, D_MODEL),
            jnp.stack(outs[0]), jnp.stack(outs[1]), jnp.stack(outs[2]),
            jnp.stack(outs[3]), jnp.stack(outs[4]), jnp.stack(outs[5]))
```

```python
import functools

import jax
import jax.numpy as jnp
from jax import lax
from jax.experimental import pallas as pl
from jax.experimental.pallas import tpu as pltpu

F32 = jnp.float32
BF16 = jnp.bfloat16

D_MODEL = 1024
CHUNK = 128
D_A = 512
G_A = 4
D_B = 512
HEAD_B = 64
H_B = D_B // HEAD_B
LORA_W = 64
LORA_A = 64
LORA_G = 128
C_A = 2 * D_A
C_RWKV = 3 * D_B + LORA_W + LORA_A + LORA_G
C_GATE = 2 * D_MODEL
C_IN = C_A + C_RWKV + C_GATE
N_GROUPS = 4
EXPERTS_PER_GROUP = 4
N_EXPERTS = N_GROUPS * EXPERTS_PER_GROUP
D_EXPERT = 256
PLE_DIM = 256
LN_EPS = 1e-5
GN_EPS = 64e-5

LANES = 128
ROUTER_LANES = LANES
RWKV_CHUNK = 64
RWKV_TILE = 256
PROJ_SUB = 128
VMEM_LIMIT = 56 * 1024 * 1024


def _ln(x, g, b, eps):
    mu = jnp.mean(x, axis=-1, keepdims=True)
    xc = x - mu
    var = jnp.mean(xc * xc, axis=-1, keepdims=True)
    return xc * lax.rsqrt(var + eps) * g + b


def _dot(a, b):
    return jnp.dot(a, b, preferred_element_type=F32)


def _split(x):
    hi = x.astype(BF16)
    lo = (x - hi.astype(F32)).astype(BF16)
    return hi, lo


def _dg(a, b, ca, cb):
    return lax.dot_general(a, b, (((ca,), (cb,)), ((), ())), preferred_element_type=F32)


def _gelu_tanh(x):
    return 0.5 * x * (1.0 + jnp.tanh(0.7978845608028654 * (x + 0.044715 * (x * x * x))))


def _sigmoid(x):
    return 1.0 / (1.0 + jnp.exp(-x))


def _softplus(x):
    return jnp.maximum(x, 0.0) + jnp.log(1.0 + jnp.exp(-jnp.abs(x)))


def _const_spec(shape):
    nd = len(shape)
    return pl.BlockSpec(shape, lambda *_: (0,) * nd, pipeline_mode=pl.Buffered(1))


def _proj_kernel(*refs, tm, sub, tiles_per_seq, single_token, norm_input):
    if single_token:
        (x_ref, lning_ref, lninb_ref, prev_ref, win_ref, bgate_ref, lnvg_ref, lnvb_ref, wmix_ref, bmix_ref,
         waout_ref, mu_ref, w0_ref, ww2_ref, a0_ref, wa2_ref, wg2_ref, kkw_ref, kaw_ref, hsum_ref,
         ga_ref, gb_ref, va_ref, last_ref, r_ref, lw_ref, k_ref, v_ref, kn_ref, ic_ref, g_ref) = refs
    else:
        (x_ref, lning_ref, lninb_ref, win_ref, bgate_ref, lnvg_ref, lnvb_ref, wmix_ref, bmix_ref,
         waout_ref, mu_ref, w0_ref, ww2_ref, a0_ref, wa2_ref, wg2_ref, kkw_ref, kaw_ref, hsum_ref,
         ga_ref, gb_ref, va_ref, last_ref, r_ref, lw_ref, k_ref, v_ref, kn_ref, ic_ref, g_ref,
         carry_ref) = refs

    n_sub = tm // sub
    rr = lax.broadcasted_iota(jnp.int32, (CHUNK, CHUNK), 0)
    cc = lax.broadcasted_iota(jnp.int32, (CHUNK, CHUNK), 1)
    wmix = [jnp.where(rr >= cc, wmix_ref[g], 0.0).astype(BF16) for g in range(G_A)]
    if not single_token:
        i = pl.program_id(0)

        @pl.when(i == 0)
        def _():
            carry_ref[...] = jnp.zeros_like(carry_ref)

        first = jnp.where(i % tiles_per_seq == 0, jnp.zeros((1, C_RWKV), F32), carry_ref[0:1, :])
        row = lax.broadcasted_iota(jnp.int32, (sub, C_RWKV), 0)

    def project(q):
        x = x_ref[q * sub:(q + 1) * sub, :]
        if norm_input:
            x = _ln(x, lning_ref[...], lninb_ref[...], LN_EPS)
        xb = x.astype(BF16)
        return (_dot(xb, win_ref[:, 0:C_A]), _dot(xb, win_ref[:, C_A + C_RWKV:]),
                _dot(xb, win_ref[:, C_A:C_A + C_RWKV]))

    def finish(q, pa, pg, pb, prev_row):
        rows = slice(q * sub, (q + 1) * sub)
        z = _gelu_tanh(pa)
        u = z[:, :D_A]
        va = _ln(z[:, D_A:], lnvg_ref[...], lnvb_ref[...], LN_EPS)
        va_ref[rows, :] = va
        vab = va.astype(BF16)
        mix = jnp.concatenate(
            [jnp.concatenate([_dot(wmix[g], vab[c * CHUNK:(c + 1) * CHUNK, g * LANES:(g + 1) * LANES])
                              for g in range(G_A)], axis=1) + bmix_ref[...]
             for c in range(sub // CHUNK)], axis=0)
        ao = _dot((u * mix).astype(BF16), waout_ref[...])
        gates = _sigmoid(pg + bgate_ref[...])
        ga_ref[rows, :] = (gates[:, :D_MODEL] * ao).astype(BF16)
        gb_ref[rows, :] = gates[:, D_MODEL:].astype(BF16)

        if single_token:
            prev = prev_ref[rows, :]
        else:
            prev = jnp.where(row == 0, prev_row, pltpu.roll(pb, 1, 0))
        xs = pb + mu_ref[...] * (prev - pb)
        r = xs[:, 0:D_B]
        k = xs[:, D_B:2 * D_B]
        v = xs[:, 2 * D_B:3 * D_B]
        o = 3 * D_B
        wa_d = xs[:, o:o + LORA_W + LORA_A]
        gd = xs[:, o + LORA_W + LORA_A:]
        w_log = -_softplus(-(w0_ref[...] + _dot(jnp.tanh(wa_d).astype(BF16), ww2_ref[...]))) - 0.5
        lw_ref[rows, :] = -jnp.exp(w_log)
        iclr = _sigmoid(a0_ref[...] + _dot(wa_d.astype(BF16), wa2_ref[...]))
        g_ref[rows, :] = _dot(_sigmoid(gd).astype(BF16), wg2_ref[...])
        kk = k * kkw_ref[...]
        sq_hi, sq_lo = _split(kk * kk)
        ss = _dot(sq_hi, hsum_ref[...]) + _dot(sq_lo, hsum_ref[...])
        kn_ref[rows, :] = kk / jnp.maximum(jnp.sqrt(ss), 1e-12)
        k_ref[rows, :] = k * (1.0 + (iclr - 1.0) * kaw_ref[...])
        r_ref[rows, :] = r
        v_ref[rows, :] = v
        ic_ref[rows, :] = iclr

    proj = [project(0)]
    for q in range(n_sub):
        if q + 1 < n_sub:
            proj.append(project(q + 1))
        prev_row = None
        if not single_token:
            prev_row = first if q == 0 else proj[q - 1][2][sub - 1:sub, :]
        finish(q, *proj[q], prev_row)
    pb_last = proj[n_sub - 1][2]
    if single_token:
        last_ref[...] = pb_last
    else:
        carry_ref[0:1, :] = pb_last[sub - 1:sub, :]
        last_ref[0] = pb_last[sub - 8:sub, :]


def _proj_call(x, prev, layer, lw, tm, sub, tiles_per_seq, single_token, norm_input):
    m = x.shape[0]
    nt = m // tm
    row = lambda n: pl.BlockSpec((tm, n), lambda i: (i, 0))
    in_specs = [row(D_MODEL), _const_spec((1, D_MODEL)), _const_spec((1, D_MODEL))]
    args = [x, lw["ln_in_g"], lw["ln_in_b"]]
    if single_token:
        in_specs.append(pl.BlockSpec((None, tm, C_RWKV), lambda i: (layer, i, 0)))
        args.append(prev)
    weights = [lw["w_in"], lw["b_gate"], lw["ln_v_g"], lw["ln_v_b"], lw["wmix"], lw["bmix"], lw["w_a_out"],
               lw["mu_shift"], lw["w0"], lw["w_w2p"], lw["a0"], lw["w_a2p"], lw["w_g2"], lw["k_k"], lw["k_a"],
               lw["hsum"]]
    in_specs += [_const_spec(w.shape) for w in weights]
    args += weights
    out_shape = [jax.ShapeDtypeStruct((m, D_MODEL), BF16), jax.ShapeDtypeStruct((m, D_MODEL), BF16),
                 jax.ShapeDtypeStruct((m, D_A), F32)]
    out_specs = [row(D_MODEL), row(D_MODEL), row(D_A)]
    if single_token:
        out_shape.append(jax.ShapeDtypeStruct((m, C_RWKV), F32))
        out_specs.append(row(C_RWKV))
    else:
        out_shape.append(jax.ShapeDtypeStruct((nt, 8, C_RWKV), F32))
        out_specs.append(pl.BlockSpec((1, 8, C_RWKV), lambda i: (i, 0, 0)))
    out_shape += [jax.ShapeDtypeStruct((m, D_B), F32)] * 7
    out_specs += [row(D_B)] * 7
    scratch = [] if single_token else [pltpu.VMEM((8, C_RWKV), F32)]
    return pl.pallas_call(
        functools.partial(_proj_kernel, tm=tm, sub=sub, tiles_per_seq=tiles_per_seq, single_token=single_token,
                          norm_input=norm_input),
        grid=(nt,),
        in_specs=in_specs,
        out_specs=out_specs,
        out_shape=out_shape,
        scratch_shapes=scratch,
        compiler_params=pltpu.CompilerParams(dimension_semantics=("arbitrary",), vmem_limit_bytes=VMEM_LIMIT),
        name="proj_single" if single_token else "proj_seq",
    )(*args)


def _rwkv_out(y, r, k, v, g, lnxg, lnxb, rkw):
    yn = _ln(y, lnxg, lnxb, GN_EPS)
    bonus = jnp.sum(r * k * rkw, axis=-1, keepdims=True) * v
    return (yn + bonus) * g


def _rwkv_tile_kernel(r_ref, lw_ref, k_ref, v_ref, kn_ref, ic_ref, g_ref, lnxg_ref, lnxb_ref, rkw_ref, hh_ref,
                      y_ref, sfin_ref, state_ref):
    ti = pl.program_id(1)
    n_t = pl.num_programs(1)
    C = RWKV_CHUNK
    RT = RWKV_TILE
    n_ch = RT // C

    @pl.when(ti == 0)
    def _():
        state_ref[...] = jnp.zeros_like(state_ref)

    row = lax.broadcasted_iota(jnp.int32, (RT, RT), 0)
    col = lax.broadcasted_iota(jnp.int32, (RT, RT), 1)
    same = (row // C) == (col // C)
    incl = jnp.logical_and(same, row >= col)
    strict = jnp.logical_and(same, row > col)
    tri = jnp.where(incl, 1.0, 0.0).astype(BF16)
    eye = jnp.where(row == col, 1.0, 0.0)
    prow = lax.broadcasted_iota(jnp.int32, (LANES, LANES), 0)
    pcol = lax.broadcasted_iota(jnp.int32, (LANES, LANES), 1)
    pair_bd = (prow // HEAD_B) == (pcol // HEAD_B)
    in_first = lax.broadcasted_iota(jnp.int32, (1, LANES), 1) < HEAD_B
    first_head = jnp.where(in_first, 1.0, 0.0)
    second_head = 1.0 - first_head

    lw = lw_ref[...]
    l0 = lw.astype(BF16)
    r1 = lw - l0.astype(F32)
    l1 = r1.astype(BF16)
    l2 = (r1 - l1.astype(F32)).astype(BF16)
    cum = _dot(tri, l0) + (_dot(tri, l1) + _dot(tri, l2))
    ends = [cum[(c + 1) * C - 1:(c + 1) * C, :] for c in range(n_ch)]
    cum_end = jnp.concatenate([jnp.broadcast_to(e, (C, D_B)) for e in ends], axis=0)
    hh2 = jnp.concatenate([hh_ref[...], hh_ref[...]], axis=0)

    def head_sum(z):
        zh, zl = _split(z)
        return _dot(jnp.concatenate([zh, zl], axis=1), hh2)

    n_pair = H_B // 2
    sls = [slice(p * LANES, (p + 1) * LANES) for p in range(n_pair)]
    atm, rtm, rt, bt_b, kt_b, bh_b, kh_b, v_b = [], [], [], [], [], [], [], []
    for p in range(n_pair):
        sl = sls[p]
        cum_p = cum[:, sl]
        lw_p = lw[:, sl]
        kn_p = kn_ref[:, sl]
        b_p = kn_p * ic_ref[:, sl]
        k_p = k_ref[:, sl]
        e_inv = jnp.exp(-cum_p)
        e_hat = jnp.exp(cum_end[:, sl] - cum_p)
        at_p = -kn_p * jnp.exp(cum_p - lw_p)
        rt_p = r_ref[:, sl] * jnp.exp(cum_p)
        for hm in (first_head, second_head):
            atm.append((at_p * hm).astype(BF16))
            rtm.append((rt_p * hm).astype(BF16))
        rt.append(rt_p)
        bt_b.append((b_p * e_inv).astype(BF16))
        kt_b.append((k_p * e_inv).astype(BF16))
        bh_b.append((b_p * e_hat).astype(BF16))
        kh_b.append((k_p * e_hat).astype(BF16))
        v_b.append(v_ref[:, sl].astype(BF16))

    pr = [h // 2 for h in range(H_B)]
    xs = [jnp.where(strict, _dg(atm[h], bt_b[pr[h]], 1, 1), 0.0) for h in range(H_B)]
    ts = [eye + x for x in xs]
    xb = [x.astype(BF16) for x in xs]
    for _ in range(5):
        xb = [_dot(z, z).astype(BF16) for z in xb]
        ts = [t + _dot(t.astype(BF16), z) for t, z in zip(ts, xb)]
    tb = [t.astype(BF16) for t in ts]
    a_ak = [jnp.where(strict, _dg(atm[h], kt_b[pr[h]], 1, 1), 0.0).astype(BF16) for h in range(H_B)]
    av = [_dot(a_ak[h], v_b[pr[h]]).astype(BF16) for h in range(H_B)]
    tx = [_dot(tb[h], jnp.concatenate([av[h], atm[h]], axis=1)) for h in range(H_B)]
    a_rb = [jnp.where(incl, _dg(rtm[h], bt_b[pr[h]], 1, 1), 0.0).astype(BF16) for h in range(H_B)]
    a_rk = [jnp.where(incl, _dg(rtm[h], kt_b[pr[h]], 1, 1), 0.0).astype(BF16) for h in range(H_B)]
    rb = [_dot(a_rb[h], tx[h].astype(BF16)) for h in range(H_B)]
    rkv = [_dot(a_rk[h], v_b[pr[h]]) for h in range(H_B)]

    w2 = [jnp.where(in_first, tx[2 * p][:, :LANES], tx[2 * p + 1][:, :LANES]).astype(BF16) for p in range(n_pair)]
    a_t = [(tx[2 * p][:, LANES:] + tx[2 * p + 1][:, LANES:]).astype(BF16) for p in range(n_pair)]
    rhat = [(rt[p] + (rb[2 * p][:, LANES:] + rb[2 * p + 1][:, LANES:])).astype(BF16) for p in range(n_pair)]
    y0 = [jnp.where(in_first, rb[2 * p][:, :LANES] + rkv[2 * p], rb[2 * p + 1][:, :LANES] + rkv[2 * p + 1])
          for p in range(n_pair)]

    s = [state_ref[p] for p in range(n_pair)]
    ys = [[] for _ in range(n_pair)]
    for c in range(n_ch):
        rows = slice(c * C, (c + 1) * C)
        el = [jnp.where(pair_bd, _dg(a_t[p][rows], bh_b[p][rows], 0, 0), 0.0).astype(BF16) for p in range(n_pair)]
        gm = [jnp.where(pair_bd,
                        _dg(jnp.concatenate([w2[p][rows], v_b[p][rows]], axis=0),
                            jnp.concatenate([bh_b[p][rows], kh_b[p][rows]], axis=0), 0, 0), 0.0)
              for p in range(n_pair)]
        sb = [z.astype(BF16) for z in s]
        for p in range(n_pair):
            ys[p].append(_dg(rhat[p][rows], sb[p], 1, 1) + y0[p][rows])
        s = [s[p] * jnp.exp(ends[c][:, sls[p]]) + _dot(sb[p], el[p]) + gm[p] for p in range(n_pair)]
    for p in range(n_pair):
        state_ref[p] = s[p]

    for p in range(n_pair):
        sl = sls[p]
        y = jnp.concatenate(ys[p], axis=0)
        mu = head_sum(y) * (1.0 / HEAD_B)
        yc = y - mu
        var = head_sum(yc * yc) * (1.0 / HEAD_B)
        yn = yc * lax.rsqrt(var + GN_EPS) * lnxg_ref[:, sl] + lnxb_ref[:, sl]
        v_p = v_ref[:, sl]
        bonus = head_sum(r_ref[:, sl] * k_ref[:, sl] * rkw_ref[:, sl]) * v_p
        y_ref[:, sl] = (yn + bonus) * g_ref[:, sl]

    @pl.when(ti == n_t - 1)
    def _():
        sfin_ref[0] = state_ref[...]


def _rwkv_seq_call(ops, lw, batch, seq):
    RT = RWKV_TILE
    n_t = seq // RT
    n_pair = H_B // 2
    blk = pl.BlockSpec((RT, D_B), lambda b, t: (b * n_t + t, 0))
    vec = pl.BlockSpec((1, D_B), lambda b, t: (0, 0))
    y, s_bd = pl.pallas_call(
        _rwkv_tile_kernel,
        grid=(batch, n_t),
        in_specs=[blk] * 7 + [vec] * 3 + [pl.BlockSpec((LANES, LANES), lambda b, t: (0, 0))],
        out_specs=[blk, pl.BlockSpec((1, n_pair, LANES, LANES), lambda b, t: (b, 0, 0, 0))],
        out_shape=[jax.ShapeDtypeStruct((batch * seq, D_B), F32),
                   jax.ShapeDtypeStruct((batch, n_pair, LANES, LANES), F32)],
        scratch_shapes=[pltpu.VMEM((n_pair, LANES, LANES), F32)],
        compiler_params=pltpu.CompilerParams(dimension_semantics=("arbitrary", "arbitrary"),
                                             vmem_limit_bytes=VMEM_LIMIT),
        name="rwkv_seq",
    )(*ops, lw["lnx_g"], lw["lnx_b"], lw["r_k"], lw["hsum"][:LANES, :LANES])
    s_bd = s_bd.reshape(batch, n_pair, 2, HEAD_B, 2, HEAD_B)
    s_fin = jnp.stack([s_bd[:, :, 0, :, 0, :], s_bd[:, :, 1, :, 1, :]], axis=2)
    return y, s_fin.reshape(batch, H_B, HEAD_B, HEAD_B)


def _rwkv_step_kernel(s_ref, r_ref, lw_ref, k_ref, v_ref, kn_ref, ic_ref, g_ref, lnxg_ref, lnxb_ref, rkw_ref,
                      y_ref, so_ref):
    n = HEAD_B
    eye = jnp.where(lax.broadcasted_iota(jnp.int32, (n, n), 0) == lax.broadcasted_iota(jnp.int32, (n, n), 1),
                    1.0, 0.0)
    s = s_ref[...]
    r = r_ref[...]
    k = k_ref[...]
    v = v_ref[...]
    kn = kn_ref[...]
    w = jnp.exp(lw_ref[...])
    sa = jnp.sum(s * (-kn), axis=-1, keepdims=True)
    v_col = jnp.sum(eye * v, axis=-1, keepdims=True)
    s1 = s * w + sa * (kn * ic_ref[...]) + v_col * k
    so_ref[...] = s1
    y_col = jnp.sum(s1 * r, axis=-1, keepdims=True)
    y = jnp.sum(eye * y_col, axis=-2, keepdims=True)
    y_ref[...] = _rwkv_out(y, r, k, v, g_ref[...], lnxg_ref[...], lnxb_ref[...], rkw_ref[...])


def _rwkv_step_call(state, layer, ops, lw, bb):
    nb = state.shape[1]
    hv = lambda z: z.reshape(-1, H_B, 1, HEAD_B)
    sblk = pl.BlockSpec((bb, H_B, HEAD_B, HEAD_B), lambda i: (i, 0, 0, 0))
    vblk = pl.BlockSpec((bb, H_B, 1, HEAD_B), lambda i: (i, 0, 0, 0))
    pblk = pl.BlockSpec((1, H_B, 1, HEAD_B), lambda i: (0, 0, 0, 0))
    y, s1 = pl.pallas_call(
        _rwkv_step_kernel,
        grid=(nb // bb,),
        in_specs=[pl.BlockSpec((None, bb, H_B, HEAD_B, HEAD_B), lambda i: (layer, i, 0, 0, 0))]
                 + [vblk] * 7 + [pblk] * 3,
        out_specs=[vblk, sblk],
        out_shape=[jax.ShapeDtypeStruct((nb, H_B, 1, HEAD_B), F32),
                   jax.ShapeDtypeStruct((nb, H_B, HEAD_B, HEAD_B), F32)],
        compiler_params=pltpu.CompilerParams(dimension_semantics=("parallel",)),
        name="rwkv_step",
    )(state, *[hv(z) for z in ops], hv(lw["lnx_g"]), hv(lw["lnx_b"]), hv(lw["r_k"]))
    return y.reshape(nb, D_B), s1


def _route(lg, tm):
    lane = lax.broadcasted_iota(jnp.int32, (tm, ROUTER_LANES), 1)
    lanef = lane.astype(F32)
    big = jnp.float32(1e9)
    ninf = jnp.float32(-jnp.inf)
    isg = lane < N_GROUPS
    lgm = jnp.where(isg, lg, ninf)
    gmax = jnp.max(lgm, axis=-1, keepdims=True)
    den = jnp.sum(jnp.exp(lgm - gmax), axis=-1, keepdims=True)
    gp = 1.0 / den
    gi = jnp.min(jnp.where(lgm == gmax, lanef, big), axis=-1, keepdims=True)
    grp = ((lane - N_GROUPS) >> 2).astype(F32)
    ise = jnp.where(lane >= N_GROUPS, grp, big) == gi
    le = jnp.where(ise, lg, ninf)
    m1 = jnp.max(le, axis=-1, keepdims=True)
    i1 = jnp.min(jnp.where(le == m1, lanef, big), axis=-1, keepdims=True)
    le2 = jnp.where(lanef == i1, ninf, le)
    m2 = jnp.max(le2, axis=-1, keepdims=True)
    i2 = jnp.min(jnp.where(le2 == m2, lanef, big), axis=-1, keepdims=True)
    e2 = jnp.exp(m2 - m1)
    w1 = 1.0 / (1.0 + e2)
    w2 = e2 / (1.0 + e2)
    return jnp.where(lanef == i1, w1 * gp, 0.0) + jnp.where(lanef == i2, w2 * gp, 0.0)


def _mix_kernel(x_ref, lning_ref, lninb_ref, ga_ref, gb_ref, yb_ref, pe_ref, wbout_ref, wo_ref, ln1g_ref,
                ln1b_ref, wr_ref, br_ref, weg_ref, weu_ref, wed_ref, wpg_ref, wpp_ref, ln2g_ref, ln2b_ref,
                o_ref, *, tm, dn_alpha, norm_input):
    x = x_ref[...]
    if norm_input:
        x = _ln(x, lning_ref[...], lninb_ref[...], LN_EPS)
    bo = _dot(yb_ref[...].astype(BF16), wbout_ref[...])
    merged = ga_ref[...] + gb_ref[...] * bo
    mix = _dot(merged.astype(BF16), wo_ref[...])
    x1 = _ln(dn_alpha * x + mix, ln1g_ref[...], ln1b_ref[...], LN_EPS)
    x1b = x1.astype(BF16)

    x1h, x1l = _split(x1)
    wrh = wr_ref[0]
    wrl = wr_ref[1]
    lg = _dot(x1h, wrh) + (_dot(x1h, wrl) + _dot(x1l, wrh)) + br_ref[...]
    comb = _route(lg, tm)

    moe = jnp.zeros((tm, D_MODEL), F32)
    for grp in range(N_GROUPS):
        hs = []
        for j in range(EXPERTS_PER_GROUP):
            e = grp * EXPERTS_PER_GROUP + j
            hg = _dot(x1b, weg_ref[e])
            hu = _dot(x1b, weu_ref[e])
            ce = comb[:, N_GROUPS + e:N_GROUPS + e + 1]
            hs.append(((hg * _sigmoid(hg)) * hu * ce).astype(BF16))
        hcat = jnp.concatenate(hs, axis=1)
        rows = slice(grp * EXPERTS_PER_GROUP * D_EXPERT, (grp + 1) * EXPERTS_PER_GROUP * D_EXPERT)
        moe = moe + _dot(hcat, wed_ref[rows, :])

    ple = _sigmoid(_dot(x1b, wpg_ref[...])) * _dot(pe_ref[...].astype(BF16), wpp_ref[...])
    o_ref[...] = _ln(dn_alpha * x1 + moe + ple, ln2g_ref[...], ln2b_ref[...], LN_EPS)


def _mix_call(x, ga, gb, yb, pe, layer, lw, tm, dn_alpha, norm_input):
    m = x.shape[0]
    row = lambda n: pl.BlockSpec((tm, n), lambda i: (i, 0))
    weights = [lw["w_b_out"], lw["w_o"], lw["ln1_g"], lw["ln1_b"], lw["w_router"], lw["b_router"],
               lw["w_e_gate"], lw["w_e_up"], lw["w_e_down"], lw["w_pe_gate"], lw["w_pe_proj"],
               lw["ln2_g"], lw["ln2_b"]]
    return pl.pallas_call(
        functools.partial(_mix_kernel, tm=tm, dn_alpha=dn_alpha, norm_input=norm_input),
        grid=(m // tm,),
        in_specs=[row(D_MODEL), _const_spec((1, D_MODEL)), _const_spec((1, D_MODEL)), row(D_MODEL), row(D_MODEL),
                  row(D_B), pl.BlockSpec((None, tm, PLE_DIM), lambda i: (layer, i, 0))]
                 + [_const_spec(w.shape) for w in weights],
        out_specs=row(D_MODEL),
        out_shape=jax.ShapeDtypeStruct((m, D_MODEL), F32),
        compiler_params=pltpu.CompilerParams(dimension_semantics=("parallel",), vmem_limit_bytes=VMEM_LIMIT),
        name="mix_moe",
    )(x, lw["ln_in_g"], lw["ln_in_b"], ga, gb, yb, pe, *weights)


def _prep_layer(i, p):
    vec = lambda z: z.reshape(1, -1)
    bf = lambda z: z.astype(BF16)
    zeros64 = jnp.zeros((LORA_W, D_B), F32)
    w_router = jnp.concatenate(
        [p["w_rg"][i], jnp.transpose(p["w_re"][i], (1, 0, 2)).reshape(D_MODEL, N_EXPERTS),
         jnp.zeros((D_MODEL, ROUTER_LANES - N_GROUPS - N_EXPERTS), F32)], axis=1)
    wr_hi = w_router.astype(BF16)
    wr_lo = (w_router - wr_hi.astype(F32)).astype(BF16)
    b_router = jnp.concatenate([p["b_rg"][i], p["b_re"][i].reshape(-1),
                                jnp.zeros((ROUTER_LANES - N_GROUPS - N_EXPERTS,), F32)])
    head = jnp.arange(D_B) // HEAD_B
    return {
        "w_in": bf(p["w_in"][i]), "b_gate": vec(p["b_gate"][i]),
        "ln_v_g": vec(p["ln_v_g"][i]), "ln_v_b": vec(p["ln_v_b"][i]),
        "w_a_out": bf(p["w_a_out"][i]), "mu_shift": vec(p["mu_shift"][i]),
        "w0": vec(p["w0"][i]), "a0": vec(p["a0"][i]),
        "w_w2p": bf(jnp.concatenate([p["w_w2"][i], zeros64], axis=0)),
        "w_a2p": bf(jnp.concatenate([zeros64, p["w_a2"][i]], axis=0)),
        "w_g2": bf(p["w_g2"][i]), "k_k": vec(p["k_k"][i]), "k_a": vec(p["k_a"][i]),
        "hsum": (head[:, None] == head[None, :]).astype(BF16),
        "lnx_g": vec(p["lnx_g"][i]), "lnx_b": vec(p["lnx_b"][i]), "r_k": vec(p["r_k"][i]),
        "w_b_out": bf(p["w_b_out"][i]), "w_o": bf(p["w_o"][i]),
        "ln1_g": vec(p["ln1_g"][i]), "ln1_b": vec(p["ln1_b"][i]),
        "w_router": jnp.stack([wr_hi, wr_lo]), "b_router": vec(b_router),
        "w_e_gate": bf(p["w_e_gate"][i]), "w_e_up": bf(p["w_e_up"][i]),
        "w_e_down": bf(p["w_e_down"][i]).reshape(N_EXPERTS * D_EXPERT, D_MODEL),
        "w_pe_gate": bf(p["w_pe_gate"][i]), "w_pe_proj": bf(p["w_pe_proj"][i]),
        "ln2_g": vec(p["ln2_g"][i]), "ln2_b": vec(p["ln2_b"][i]),
    }


def kernel(x_prompt, x_sample, state_rwkv, state_shift, p_prompt, p_sample, ln_in_g, ln_in_b, w_in, b_gate,
           ln_v_g, ln_v_b, w_s, b_s, w_a_out, mu_shift, w0, w_w2, a0, w_a2, w_g2, k_k, k_a, r_k, lnx_g, lnx_b,
           w_b_out, w_o, ln1_g, ln1_b, w_rg, b_rg, w_re, b_re, w_e_gate, w_e_up, w_e_down, w_pe_gate,
           w_pe_proj, ln2_g, ln2_b):
    p = dict(w_in=w_in, b_gate=b_gate, ln_v_g=ln_v_g, ln_v_b=ln_v_b, w_a_out=w_a_out, mu_shift=mu_shift,
             w0=w0, w_w2=w_w2, a0=a0, w_a2=w_a2, w_g2=w_g2, k_k=k_k, k_a=k_a, r_k=r_k, lnx_g=lnx_g,
             lnx_b=lnx_b, w_b_out=w_b_out, w_o=w_o, ln1_g=ln1_g, ln1_b=ln1_b, w_rg=w_rg, b_rg=b_rg,
             w_re=w_re, b_re=b_re, w_e_gate=w_e_gate, w_e_up=w_e_up, w_e_down=w_e_down,
             w_pe_gate=w_pe_gate, w_pe_proj=w_pe_proj, ln2_g=ln2_g, ln2_b=ln2_b)
    batch, seq, _ = x_prompt.shape
    nb, dec_seq, _ = x_sample.shape
    depth = w_in.shape[0]
    assert dec_seq == 1 and seq % CHUNK == 0 and seq % RWKV_TILE == 0
    dn_alpha = (2 * depth) ** 0.25
    tm = 512
    tms = nb
    tiles_per_seq = seq // tm
    last_chunk = ((seq - 1) // CHUNK) * CHUNK

    xp = x_prompt.reshape(batch * seq, D_MODEL)
    xs = x_sample.reshape(nb, D_MODEL)
    pe_p = p_prompt.reshape(depth, batch * seq, PLE_DIM)
    pe_s = p_sample.reshape(depth, nb, PLE_DIM)
    ln_in = {"ln_in_g": ln_in_g.reshape(1, D_MODEL), "ln_in_b": ln_in_b.reshape(1, D_MODEL)}

    eye = jnp.eye(CHUNK, dtype=F32)
    outs = [[] for _ in range(6)]
    for i in range(depth):
        lw = dict(_prep_layer(i, p), **ln_in)
        first = i == 0
        lw_p = dict(lw, wmix=w_s[i], bmix=jnp.repeat(jnp.transpose(b_s[i]), LANES, axis=1))
        lw_s = dict(lw, wmix=w_s[i][:, 0, 0][:, None, None] * eye[None],
                    bmix=jnp.broadcast_to(jnp.repeat(b_s[i][:, 0], LANES)[None, :], (CHUNK, D_A)))

        res = _proj_call(xp, None, i, lw_p, tm, PROJ_SUB, tiles_per_seq, False, first)
        ga, gb, va, last = res[:4]
        yb, s_fin = _rwkv_seq_call(res[4:], lw, batch, seq)
        xp = _mix_call(xp, ga, gb, yb, pe_p, i, lw, tm, dn_alpha, first)
        outs[0].append(s_fin)
        outs[1].append(last[tiles_per_seq - 1::tiles_per_seq, 7, :])
        outs[2].append(va.reshape(batch, seq, D_A)[:, last_chunk:])

        res = _proj_call(xs, state_shift, i, lw_s, tms, tms, 1, True, first)
        ga, gb, va, pb = res[:4]
        yb, s1 = _rwkv_step_call(state_rwkv, i, res[4:], lw, 8)
        xs = _mix_call(xs, ga, gb, yb, pe_s, i, lw, tms, dn_alpha, first)
        outs[3].append(s1)
        outs[4].append(pb)
        outs[5].append(va.reshape(nb, 1, D_A))

    return (xp.reshape(batch, seq, D_MODEL), xs.reshape(nb, 1, D_MODEL),
            jnp.stack(outs[0]), jnp.stack(outs[1]), jnp.stack(outs[2]),
            jnp.stack(outs[3]), jnp.stack(outs[4]), jnp.stack(outs[5]))
```

```python
import functools

import jax
import jax.numpy as jnp
from jax import lax
from jax.experimental import pallas as pl
from jax.experimental.pallas import tpu as pltpu

F32 = jnp.float32
BF16 = jnp.bfloat16

D_MODEL = 1024
CHUNK = 128
D_A = 512
G_A = 4
D_B = 512
HEAD_B = 64
H_B = D_B // HEAD_B
LORA_W = 64
LORA_A = 64
LORA_G = 128
C_A = 2 * D_A
C_RWKV = 3 * D_B + LORA_W + LORA_A + LORA_G
C_GATE = 2 * D_MODEL
C_IN = C_A + C_RWKV + C_GATE
N_GROUPS = 4
EXPERTS_PER_GROUP = 4
N_EXPERTS = N_GROUPS * EXPERTS_PER_GROUP
D_EXPERT = 256
PLE_DIM = 256
LN_EPS = 1e-5
GN_EPS = 64e-5

LANES = 128
ROUTER_LANES = LANES
RWKV_CHUNK = 64
RWKV_TILE = 256
PROJ_SUB = 128
VMEM_LIMIT = 56 * 1024 * 1024


def _ln(x, g, b, eps):
    mu = jnp.mean(x, axis=-1, keepdims=True)
    xc = x - mu
    var = jnp.mean(xc * xc, axis=-1, keepdims=True)
    return xc * lax.rsqrt(var + eps) * g + b


def _dot(a, b):
    return jnp.dot(a, b, preferred_element_type=F32)


def _split(x):
    hi = x.astype(BF16)
    lo = (x - hi.astype(F32)).astype(BF16)
    return hi, lo


def _dg(a, b, ca, cb):
    return lax.dot_general(a, b, (((ca,), (cb,)), ((), ())), preferred_element_type=F32)


def _gelu_tanh(x):
    return 0.5 * x * (1.0 + jnp.tanh(0.7978845608028654 * (x + 0.044715 * (x * x * x))))


def _sigmoid(x):
    return 1.0 / (1.0 + jnp.exp(-x))


def _softplus(x):
    return jnp.maximum(x, 0.0) + jnp.log(1.0 + jnp.exp(-jnp.abs(x)))


def _const_spec(shape):
    nd = len(shape)
    return pl.BlockSpec(shape, lambda *_: (0,) * nd, pipeline_mode=pl.Buffered(1))


def _layer_spec(arr, layer):
    nd = arr.ndim - 1
    return pl.BlockSpec((None,) + arr.shape[1:], lambda *_: (layer,) + (0,) * nd, pipeline_mode=pl.Buffered(1))


def _proj_kernel(*refs, tm, sub, tiles_per_seq, single_token, norm_input):
    if single_token:
        (x_ref, lning_ref, lninb_ref, prev_ref, win_ref, bgate_ref, lnvg_ref, lnvb_ref, wmix_ref, bmix_ref,
         waout_ref, mu_ref, w0_ref, ww2_ref, a0_ref, wa2_ref, wg2_ref, kkw_ref, kaw_ref, hsum_ref,
         ga_ref, gb_ref, va_ref, last_ref, r_ref, lw_ref, k_ref, v_ref, kn_ref, ic_ref, g_ref) = refs
    else:
        (x_ref, lning_ref, lninb_ref, win_ref, bgate_ref, lnvg_ref, lnvb_ref, wmix_ref, bmix_ref,
         waout_ref, mu_ref, w0_ref, ww2_ref, a0_ref, wa2_ref, wg2_ref, kkw_ref, kaw_ref, hsum_ref,
         ga_ref, gb_ref, va_ref, last_ref, r_ref, lw_ref, k_ref, v_ref, kn_ref, ic_ref, g_ref,
         carry_ref) = refs

    n_sub = tm // sub
    rr = lax.broadcasted_iota(jnp.int32, (CHUNK, CHUNK), 0)
    cc = lax.broadcasted_iota(jnp.int32, (CHUNK, CHUNK), 1)
    wmix = [jnp.where(rr >= cc, wmix_ref[g], 0.0).astype(BF16) for g in range(G_A)]
    if not single_token:
        i = pl.program_id(0)

        @pl.when(i == 0)
        def _():
            carry_ref[...] = jnp.zeros_like(carry_ref)

        first = jnp.where(i % tiles_per_seq == 0, jnp.zeros((1, C_RWKV), F32), carry_ref[0:1, :])
        row = lax.broadcasted_iota(jnp.int32, (sub, C_RWKV), 0)

    def project(q):
        x = x_ref[q * sub:(q + 1) * sub, :]
        if norm_input:
            x = _ln(x, lning_ref[...], lninb_ref[...], LN_EPS)
        xb = x.astype(BF16)
        return (_dot(xb, win_ref[:, 0:C_A]), _dot(xb, win_ref[:, C_A + C_RWKV:]),
                _dot(xb, win_ref[:, C_A:C_A + C_RWKV]))

    def finish(q, pa, pg, pb, prev_row):
        rows = slice(q * sub, (q + 1) * sub)
        z = _gelu_tanh(pa)
        u = z[:, :D_A]
        va = _ln(z[:, D_A:], lnvg_ref[...], lnvb_ref[...], LN_EPS)
        va_ref[rows, :] = va
        vab = va.astype(BF16)
        mix = jnp.concatenate(
            [jnp.concatenate([_dot(wmix[g], vab[c * CHUNK:(c + 1) * CHUNK, g * LANES:(g + 1) * LANES])
                              for g in range(G_A)], axis=1) + bmix_ref[...]
             for c in range(sub // CHUNK)], axis=0)
        ao = _dot((u * mix).astype(BF16), waout_ref[...])
        gates = _sigmoid(pg + bgate_ref[...])
        ga_ref[rows, :] = (gates[:, :D_MODEL] * ao).astype(BF16)
        gb_ref[rows, :] = gates[:, D_MODEL:].astype(BF16)

        if single_token:
            prev = prev_ref[rows, :]
        else:
            prev = jnp.where(row == 0, prev_row, pltpu.roll(pb, 1, 0))
        xs = pb + mu_ref[...] * (prev - pb)
        r = xs[:, 0:D_B]
        k = xs[:, D_B:2 * D_B]
        v = xs[:, 2 * D_B:3 * D_B]
        o = 3 * D_B
        wa_d = xs[:, o:o + LORA_W + LORA_A]
        gd = xs[:, o + LORA_W + LORA_A:]
        w_log = -_softplus(-(w0_ref[...] + _dot(jnp.tanh(wa_d).astype(BF16), ww2_ref[...]))) - 0.5
        iclr = _sigmoid(a0_ref[...] + _dot(wa_d.astype(BF16), wa2_ref[...]))
        kk = k * kkw_ref[...]
        sq_hi, sq_lo = _split(kk * kk)
        ss = _dot(sq_hi, hsum_ref[...]) + _dot(sq_lo, hsum_ref[...])
        outs = ((lw_ref, -jnp.exp(w_log)), (g_ref, _dot(_sigmoid(gd).astype(BF16), wg2_ref[...])),
                (kn_ref, kk / jnp.maximum(jnp.sqrt(ss), 1e-12)), (k_ref, k * (1.0 + (iclr - 1.0) * kaw_ref[...])),
                (r_ref, r), (v_ref, v), (ic_ref, iclr))
        if single_token:
            t = jnp.concatenate([val for _, val in outs], axis=1).T
            for j, (ref, _) in enumerate(outs):
                ref[:, rows] = t[j * D_B:(j + 1) * D_B, :]
        else:
            for ref, val in outs:
                ref[rows, :] = val

    proj = [project(0)]
    for q in range(n_sub):
        if q + 1 < n_sub:
            proj.append(project(q + 1))
        prev_row = None
        if not single_token:
            prev_row = first if q == 0 else proj[q - 1][2][sub - 1:sub, :]
        finish(q, *proj[q], prev_row)
    pb_last = proj[n_sub - 1][2]
    if single_token:
        last_ref[...] = pb_last
    else:
        carry_ref[0:1, :] = pb_last[sub - 1:sub, :]
        last_ref[0] = pb_last[sub - 8:sub, :]


def _proj_call(x, prev, layer, lw, tm, sub, tiles_per_seq, single_token, norm_input):
    m = x.shape[0]
    nt = m // tm
    row = lambda n: pl.BlockSpec((tm, n), lambda i: (i, 0))
    in_specs = [row(D_MODEL), _const_spec((1, D_MODEL)), _const_spec((1, D_MODEL))]
    args = [x, lw["ln_in_g"], lw["ln_in_b"]]
    if single_token:
        in_specs.append(pl.BlockSpec((None, tm, C_RWKV), lambda i: (layer, i, 0)))
        args.append(prev)
    weights = [lw[n] for n in ("w_in", "b_gate", "ln_v_g", "ln_v_b", "wmix", "bmix", "w_a_out", "mu_shift", "w0",
                               "w_w2p", "a0", "w_a2p", "w_g2", "k_k", "k_a")]
    in_specs += [_layer_spec(w, layer) for w in weights] + [_const_spec(lw["hsum"].shape)]
    args += weights + [lw["hsum"]]
    out_shape = [jax.ShapeDtypeStruct((m, D_MODEL), BF16), jax.ShapeDtypeStruct((m, D_MODEL), BF16),
                 jax.ShapeDtypeStruct((m, D_A), F32)]
    out_specs = [row(D_MODEL), row(D_MODEL), row(D_A)]
    if single_token:
        out_shape.append(jax.ShapeDtypeStruct((m, C_RWKV), F32))
        out_specs.append(row(C_RWKV))
    else:
        out_shape.append(jax.ShapeDtypeStruct((nt, 8, C_RWKV), F32))
        out_specs.append(pl.BlockSpec((1, 8, C_RWKV), lambda i: (i, 0, 0)))
    if single_token:
        out_shape += [jax.ShapeDtypeStruct((D_B, m), F32)] * 7
        out_specs += [pl.BlockSpec((D_B, tm), lambda i: (0, i))] * 7
    else:
        out_shape += [jax.ShapeDtypeStruct((m, D_B), F32)] * 7
        out_specs += [row(D_B)] * 7
    scratch = [] if single_token else [pltpu.VMEM((8, C_RWKV), F32)]
    return pl.pallas_call(
        functools.partial(_proj_kernel, tm=tm, sub=sub, tiles_per_seq=tiles_per_seq, single_token=single_token,
                          norm_input=norm_input),
        grid=(nt,),
        in_specs=in_specs,
        out_specs=out_specs,
        out_shape=out_shape,
        scratch_shapes=scratch,
        compiler_params=pltpu.CompilerParams(dimension_semantics=("arbitrary",), vmem_limit_bytes=VMEM_LIMIT),
        name="proj_single" if single_token else "proj_seq",
    )(*args)


def _rwkv_tile_kernel(r_ref, lw_ref, k_ref, v_ref, kn_ref, ic_ref, g_ref, lnxg_ref, lnxb_ref, rkw_ref, hh_ref,
                      y_ref, sfin_ref, state_ref):
    ti = pl.program_id(1)
    n_t = pl.num_programs(1)
    C = RWKV_CHUNK
    RT = RWKV_TILE
    n_ch = RT // C

    @pl.when(ti == 0)
    def _():
        state_ref[...] = jnp.zeros_like(state_ref)

    row = lax.broadcasted_iota(jnp.int32, (RT, RT), 0)
    col = lax.broadcasted_iota(jnp.int32, (RT, RT), 1)
    same = (row // C) == (col // C)
    incl = jnp.logical_and(same, row >= col)
    strict = jnp.logical_and(same, row > col)
    tri = jnp.where(incl, 1.0, 0.0).astype(BF16)
    eye = jnp.where(row == col, 1.0, 0.0)
    prow = lax.broadcasted_iota(jnp.int32, (LANES, LANES), 0)
    pcol = lax.broadcasted_iota(jnp.int32, (LANES, LANES), 1)
    pair_bd = (prow // HEAD_B) == (pcol // HEAD_B)
    in_first = lax.broadcasted_iota(jnp.int32, (1, LANES), 1) < HEAD_B
    first_head = jnp.where(in_first, 1.0, 0.0)
    second_head = 1.0 - first_head

    lw = lw_ref[...]
    l0 = lw.astype(BF16)
    r1 = lw - l0.astype(F32)
    l1 = r1.astype(BF16)
    l2 = (r1 - l1.astype(F32)).astype(BF16)
    cum = _dot(tri, l0) + (_dot(tri, l1) + _dot(tri, l2))
    ends = [cum[(c + 1) * C - 1:(c + 1) * C, :] for c in range(n_ch)]
    cum_end = jnp.concatenate([jnp.broadcast_to(e, (C, D_B)) for e in ends], axis=0)
    hh2 = jnp.concatenate([hh_ref[...], hh_ref[...]], axis=0)

    def head_sum(z):
        zh, zl = _split(z)
        return _dot(jnp.concatenate([zh, zl], axis=1), hh2)

    n_pair = H_B // 2
    sls = [slice(p * LANES, (p + 1) * LANES) for p in range(n_pair)]
    atm, rtm, rt, bt_b, kt_b, bh_b, kh_b, v_b = [], [], [], [], [], [], [], []
    for p in range(n_pair):
        sl = sls[p]
        cum_p = cum[:, sl]
        lw_p = lw[:, sl]
        kn_p = kn_ref[:, sl]
        b_p = kn_p * ic_ref[:, sl]
        k_p = k_ref[:, sl]
        e_inv = jnp.exp(-cum_p)
        e_hat = jnp.exp(cum_end[:, sl] - cum_p)
        at_p = -kn_p * jnp.exp(cum_p - lw_p)
        rt_p = r_ref[:, sl] * jnp.exp(cum_p)
        for hm in (first_head, second_head):
            atm.append((at_p * hm).astype(BF16))
            rtm.append((rt_p * hm).astype(BF16))
        rt.append(rt_p)
        bt_b.append((b_p * e_inv).astype(BF16))
        kt_b.append((k_p * e_inv).astype(BF16))
        bh_b.append((b_p * e_hat).astype(BF16))
        kh_b.append((k_p * e_hat).astype(BF16))
        v_b.append(v_ref[:, sl].astype(BF16))

    pr = [h // 2 for h in range(H_B)]
    xs = [jnp.where(strict, _dg(atm[h], bt_b[pr[h]], 1, 1), 0.0) for h in range(H_B)]
    ts = [eye + x for x in xs]
    xb = [x.astype(BF16) for x in xs]
    for _ in range(5):
        xb = [_dot(z, z).astype(BF16) for z in xb]
        ts = [t + _dot(t.astype(BF16), z) for t, z in zip(ts, xb)]
    tb = [t.astype(BF16) for t in ts]
    a_ak = [jnp.where(strict, _dg(atm[h], kt_b[pr[h]], 1, 1), 0.0).astype(BF16) for h in range(H_B)]
    av = [_dot(a_ak[h], v_b[pr[h]]).astype(BF16) for h in range(H_B)]
    tx = [_dot(tb[h], jnp.concatenate([av[h], atm[h]], axis=1)) for h in range(H_B)]
    a_rb = [jnp.where(incl, _dg(rtm[h], bt_b[pr[h]], 1, 1), 0.0).astype(BF16) for h in range(H_B)]
    a_rk = [jnp.where(incl, _dg(rtm[h], kt_b[pr[h]], 1, 1), 0.0).astype(BF16) for h in range(H_B)]
    rb = [_dot(a_rb[h], tx[h].astype(BF16)) for h in range(H_B)]
    rkv = [_dot(a_rk[h], v_b[pr[h]]) for h in range(H_B)]

    w2 = [jnp.where(in_first, tx[2 * p][:, :LANES], tx[2 * p + 1][:, :LANES]).astype(BF16) for p in range(n_pair)]
    a_t = [(tx[2 * p][:, LANES:] + tx[2 * p + 1][:, LANES:]).astype(BF16) for p in range(n_pair)]
    rhat = [(rt[p] + (rb[2 * p][:, LANES:] + rb[2 * p + 1][:, LANES:])).astype(BF16) for p in range(n_pair)]
    y0 = [jnp.where(in_first, rb[2 * p][:, :LANES] + rkv[2 * p], rb[2 * p + 1][:, :LANES] + rkv[2 * p + 1])
          for p in range(n_pair)]

    s = [state_ref[p] for p in range(n_pair)]
    ys = [[] for _ in range(n_pair)]
    for c in range(n_ch):
        rows = slice(c * C, (c + 1) * C)
        el = [jnp.where(pair_bd, _dg(a_t[p][rows], bh_b[p][rows], 0, 0), 0.0).astype(BF16) for p in range(n_pair)]
        gm = [jnp.where(pair_bd,
                        _dg(jnp.concatenate([w2[p][rows], v_b[p][rows]], axis=0),
                            jnp.concatenate([bh_b[p][rows], kh_b[p][rows]], axis=0), 0, 0), 0.0)
              for p in range(n_pair)]
        sb = [z.astype(BF16) for z in s]
        for p in range(n_pair):
            ys[p].append(_dg(rhat[p][rows], sb[p], 1, 1) + y0[p][rows])
        s = [s[p] * jnp.exp(ends[c][:, sls[p]]) + _dot(sb[p], el[p]) + gm[p] for p in range(n_pair)]
    for p in range(n_pair):
        state_ref[p] = s[p]

    for p in range(n_pair):
        sl = sls[p]
        y = jnp.concatenate(ys[p], axis=0)
        mu = head_sum(y) * (1.0 / HEAD_B)
        yc = y - mu
        var = head_sum(yc * yc) * (1.0 / HEAD_B)
        yn = yc * lax.rsqrt(var + GN_EPS) * lnxg_ref[:, sl] + lnxb_ref[:, sl]
        v_p = v_ref[:, sl]
        bonus = head_sum(r_ref[:, sl] * k_ref[:, sl] * rkw_ref[:, sl]) * v_p
        y_ref[:, sl] = (yn + bonus) * g_ref[:, sl]

    @pl.when(ti == n_t - 1)
    def _():
        sfin_ref[0] = state_ref[...]


def _rwkv_seq_call(ops, layer, lw, batch, seq):
    RT = RWKV_TILE
    n_t = seq // RT
    n_pair = H_B // 2
    blk = pl.BlockSpec((RT, D_B), lambda b, t: (b * n_t + t, 0))
    vec = pl.BlockSpec((None, 1, D_B), lambda b, t: (layer, 0, 0))
    y, s_bd = pl.pallas_call(
        _rwkv_tile_kernel,
        grid=(batch, n_t),
        in_specs=[blk] * 7 + [vec] * 3 + [pl.BlockSpec((LANES, LANES), lambda b, t: (0, 0))],
        out_specs=[blk, pl.BlockSpec((1, n_pair, LANES, LANES), lambda b, t: (b, 0, 0, 0))],
        out_shape=[jax.ShapeDtypeStruct((batch * seq, D_B), F32),
                   jax.ShapeDtypeStruct((batch, n_pair, LANES, LANES), F32)],
        scratch_shapes=[pltpu.VMEM((n_pair, LANES, LANES), F32)],
        compiler_params=pltpu.CompilerParams(dimension_semantics=("arbitrary", "arbitrary"),
                                             vmem_limit_bytes=VMEM_LIMIT),
        name="rwkv_seq",
    )(*ops, lw["lnx_g"], lw["lnx_b"], lw["r_k"], lw["hsum"][:LANES, :LANES])
    s_bd = s_bd.reshape(batch, n_pair, 2, HEAD_B, 2, HEAD_B)
    s_fin = jnp.stack([s_bd[:, :, 0, :, 0, :], s_bd[:, :, 1, :, 1, :]], axis=2)
    return y, s_fin.reshape(batch, H_B, HEAD_B, HEAD_B)


def _rwkv_step_kernel(s_ref, r_ref, lw_ref, k_ref, v_ref, kn_ref, ic_ref, g_ref, lnxg_ref, lnxb_ref, rkw_ref,
                      buf_ref, y_ref, so_ref, y_scr):
    del buf_ref
    kn = kn_ref[...]
    a = -kn
    b = kn * ic_ref[...]
    w = jnp.exp(lw_ref[...])
    k = k_ref[...]
    r = r_ref[...]
    for i in range(HEAD_B):
        s = s_ref[i]
        sa = jnp.sum(s * a, axis=0, keepdims=True)
        s1 = s * w + sa * b + v_ref[i:i + 1, :] * k
        so_ref[i] = s1
        y_scr[i:i + 1, :] = jnp.sum(s1 * r, axis=0, keepdims=True)
    y = y_scr[...]
    mu = jnp.mean(y, axis=0, keepdims=True)
    yc = y - mu
    var = jnp.mean(yc * yc, axis=0, keepdims=True)
    yn = yc * lax.rsqrt(var + GN_EPS) * lnxg_ref[...] + lnxb_ref[...]
    bonus = jnp.sum(r * k * rkw_ref[...], axis=0, keepdims=True) * v_ref[...]
    y_ref[...] = (yn + bonus) * g_ref[...]


def _rwkv_step_call(state_t, buf, layer, ops_t, lw, nb):
    sblk = pl.BlockSpec((None, None, HEAD_B, HEAD_B, nb), lambda h: (layer, h, 0, 0, 0))
    vblk = pl.BlockSpec((HEAD_B, nb), lambda h: (h, 0))
    pblk = pl.BlockSpec((None, HEAD_B, nb), lambda h: (layer, h, 0))
    return pl.pallas_call(
        _rwkv_step_kernel,
        grid=(H_B,),
        in_specs=[sblk] + [vblk] * 7 + [pblk] * 3 + [pl.BlockSpec(memory_space=pl.ANY)],
        out_specs=[vblk, sblk],
        out_shape=[jax.ShapeDtypeStruct((D_B, nb), F32), jax.ShapeDtypeStruct(buf.shape, F32)],
        scratch_shapes=[pltpu.VMEM((HEAD_B, nb), F32)],
        input_output_aliases={11: 1},
        compiler_params=pltpu.CompilerParams(dimension_semantics=("arbitrary",)),
        name="rwkv_step",
    )(state_t, *ops_t, lw["lnx_g_t"], lw["lnx_b_t"], lw["r_k_t"], buf)


def _route(lg, tm):
    lane = lax.broadcasted_iota(jnp.int32, (tm, ROUTER_LANES), 1)
    lanef = lane.astype(F32)
    big = jnp.float32(1e9)
    ninf = jnp.float32(-jnp.inf)
    isg = lane < N_GROUPS
    lgm = jnp.where(isg, lg, ninf)
    gmax = jnp.max(lgm, axis=-1, keepdims=True)
    den = jnp.sum(jnp.exp(lgm - gmax), axis=-1, keepdims=True)
    gp = 1.0 / den
    gi = jnp.min(jnp.where(lgm == gmax, lanef, big), axis=-1, keepdims=True)
    grp = ((lane - N_GROUPS) >> 2).astype(F32)
    ise = jnp.where(lane >= N_GROUPS, grp, big) == gi
    le = jnp.where(ise, lg, ninf)
    m1 = jnp.max(le, axis=-1, keepdims=True)
    i1 = jnp.min(jnp.where(le == m1, lanef, big), axis=-1, keepdims=True)
    le2 = jnp.where(lanef == i1, ninf, le)
    m2 = jnp.max(le2, axis=-1, keepdims=True)
    i2 = jnp.min(jnp.where(le2 == m2, lanef, big), axis=-1, keepdims=True)
    e2 = jnp.exp(m2 - m1)
    w1 = 1.0 / (1.0 + e2)
    w2 = e2 / (1.0 + e2)
    return jnp.where(lanef == i1, w1 * gp, 0.0) + jnp.where(lanef == i2, w2 * gp, 0.0)


def _mix_kernel(x_ref, lning_ref, lninb_ref, ga_ref, gb_ref, yb_ref, pe_ref, wbout_ref, wo_ref, ln1g_ref,
                ln1b_ref, wr_ref, br_ref, weg_ref, weu_ref, wed_ref, wpg_ref, wpp_ref, ln2g_ref, ln2b_ref,
                o_ref, *, tm, dn_alpha, norm_input, yb_on_lanes):
    x = x_ref[...]
    if norm_input:
        x = _ln(x, lning_ref[...], lninb_ref[...], LN_EPS)
    if yb_on_lanes:
        bo = _dg(yb_ref[...].astype(BF16), wbout_ref[...], 0, 0)
    else:
        bo = _dot(yb_ref[...].astype(BF16), wbout_ref[...])
    merged = ga_ref[...] + gb_ref[...] * bo
    mix = _dot(merged.astype(BF16), wo_ref[...])
    x1 = _ln(dn_alpha * x + mix, ln1g_ref[...], ln1b_ref[...], LN_EPS)
    x1b = x1.astype(BF16)

    x1h, x1l = _split(x1)
    wrh = wr_ref[0]
    wrl = wr_ref[1]
    lg = _dot(x1h, wrh) + (_dot(x1h, wrl) + _dot(x1l, wrh)) + br_ref[...]
    comb = _route(lg, tm)

    moe = jnp.zeros((tm, D_MODEL), F32)
    for grp in range(N_GROUPS):
        hs = []
        for j in range(EXPERTS_PER_GROUP):
            e = grp * EXPERTS_PER_GROUP + j
            hg = _dot(x1b, weg_ref[e])
            hu = _dot(x1b, weu_ref[e])
            ce = comb[:, N_GROUPS + e:N_GROUPS + e + 1]
            hs.append(((hg * _sigmoid(hg)) * hu * ce).astype(BF16))
        hcat = jnp.concatenate(hs, axis=1)
        rows = slice(grp * EXPERTS_PER_GROUP * D_EXPERT, (grp + 1) * EXPERTS_PER_GROUP * D_EXPERT)
        moe = moe + _dot(hcat, wed_ref[rows, :])

    ple = _sigmoid(_dot(x1b, wpg_ref[...])) * _dot(pe_ref[...].astype(BF16), wpp_ref[...])
    o_ref[...] = _ln(dn_alpha * x1 + moe + ple, ln2g_ref[...], ln2b_ref[...], LN_EPS)


def _mix_call(x, ga, gb, yb, pe, layer, lw, tm, dn_alpha, norm_input, yb_on_lanes):
    m = x.shape[0]
    row = lambda n: pl.BlockSpec((tm, n), lambda i: (i, 0))
    yb_spec = pl.BlockSpec((D_B, tm), lambda i: (0, i)) if yb_on_lanes else row(D_B)
    weights = [lw[n] for n in ("w_b_out", "w_o", "ln1_g", "ln1_b", "w_router", "b_router", "w_e_gate", "w_e_up",
                               "w_e_down", "w_pe_gate", "w_pe_proj", "ln2_g", "ln2_b")]
    return pl.pallas_call(
        functools.partial(_mix_kernel, tm=tm, dn_alpha=dn_alpha, norm_input=norm_input, yb_on_lanes=yb_on_lanes),
        grid=(m // tm,),
        in_specs=[row(D_MODEL), _const_spec((1, D_MODEL)), _const_spec((1, D_MODEL)), row(D_MODEL), row(D_MODEL),
                  yb_spec, pl.BlockSpec((None, tm, PLE_DIM), lambda i: (layer, i, 0))]
                 + [_layer_spec(w, layer) for w in weights],
        out_specs=row(D_MODEL),
        out_shape=jax.ShapeDtypeStruct((m, D_MODEL), F32),
        compiler_params=pltpu.CompilerParams(dimension_semantics=("parallel",), vmem_limit_bytes=VMEM_LIMIT),
        name="mix_moe",
    )(x, lw["ln_in_g"], lw["ln_in_b"], ga, gb, yb, pe, *weights)


def _prep_weights(p, nb):
    depth = p["w_in"].shape[0]
    vec = lambda z: z.reshape(depth, 1, -1)
    bf = lambda z: z.astype(BF16)
    lanes = lambda z: jnp.broadcast_to(z.reshape(depth, D_B, 1), (depth, D_B, nb))
    zeros64 = jnp.zeros((depth, LORA_W, D_B), F32)
    pad = ROUTER_LANES - N_GROUPS - N_EXPERTS
    w_router = jnp.concatenate(
        [p["w_rg"], jnp.transpose(p["w_re"], (0, 2, 1, 3)).reshape(depth, D_MODEL, N_EXPERTS),
         jnp.zeros((depth, D_MODEL, pad), F32)], axis=2)
    wr_hi = w_router.astype(BF16)
    wr_lo = (w_router - wr_hi.astype(F32)).astype(BF16)
    b_router = jnp.concatenate([p["b_rg"], p["b_re"].reshape(depth, N_EXPERTS), jnp.zeros((depth, pad), F32)], axis=1)
    head = jnp.arange(D_B) // HEAD_B
    eye = jnp.eye(CHUNK, dtype=F32)
    w = {n: vec(p[n]) for n in ("b_gate", "ln_v_g", "ln_v_b", "mu_shift", "w0", "a0", "k_k", "k_a", "lnx_g", "lnx_b",
                                "ln1_g", "ln1_b", "ln2_g", "ln2_b")}
    w.update({n: bf(p[n]) for n in ("w_in", "w_a_out", "w_g2", "w_b_out", "w_o", "w_e_gate", "w_e_up", "w_pe_gate",
                                    "w_pe_proj")})
    w.update({
        "w_w2p": bf(jnp.concatenate([p["w_w2"], zeros64], axis=1)),
        "w_a2p": bf(jnp.concatenate([zeros64, p["w_a2"]], axis=1)),
        "hsum": (head[:, None] == head[None, :]).astype(BF16),
        "r_k": vec(p["r_k"]),
        "lnx_g_t": lanes(p["lnx_g"]), "lnx_b_t": lanes(p["lnx_b"]), "r_k_t": lanes(p["r_k"]),
        "w_router": jnp.stack([wr_hi, wr_lo], axis=1), "b_router": vec(b_router),
        "w_e_down": bf(p["w_e_down"]).reshape(depth, N_EXPERTS * D_EXPERT, D_MODEL),
        "wmix_seq": p["w_s"],
        "bmix_seq": jnp.repeat(jnp.transpose(p["b_s"], (0, 2, 1)), LANES, axis=2),
        "wmix_one": p["w_s"][:, :, 0, 0][:, :, None, None] * eye,
        "bmix_one": jnp.broadcast_to(jnp.repeat(p["b_s"][:, :, 0], LANES, axis=1)[:, None, :], (depth, CHUNK, D_A)),
    })
    return w


def kernel(x_prompt, x_sample, state_rwkv, state_shift, p_prompt, p_sample, ln_in_g, ln_in_b, w_in, b_gate,
           ln_v_g, ln_v_b, w_s, b_s, w_a_out, mu_shift, w0, w_w2, a0, w_a2, w_g2, k_k, k_a, r_k, lnx_g, lnx_b,
           w_b_out, w_o, ln1_g, ln1_b, w_rg, b_rg, w_re, b_re, w_e_gate, w_e_up, w_e_down, w_pe_gate,
           w_pe_proj, ln2_g, ln2_b):
    p = dict(w_in=w_in, b_gate=b_gate, ln_v_g=ln_v_g, ln_v_b=ln_v_b, w_s=w_s, b_s=b_s, w_a_out=w_a_out,
             mu_shift=mu_shift, w0=w0, w_w2=w_w2, a0=a0, w_a2=w_a2, w_g2=w_g2, k_k=k_k, k_a=k_a, r_k=r_k,
             lnx_g=lnx_g, lnx_b=lnx_b, w_b_out=w_b_out, w_o=w_o, ln1_g=ln1_g, ln1_b=ln1_b, w_rg=w_rg, b_rg=b_rg,
             w_re=w_re, b_re=b_re, w_e_gate=w_e_gate, w_e_up=w_e_up, w_e_down=w_e_down,
             w_pe_gate=w_pe_gate, w_pe_proj=w_pe_proj, ln2_g=ln2_g, ln2_b=ln2_b)
    batch, seq, _ = x_prompt.shape
    nb, dec_seq, _ = x_sample.shape
    depth = w_in.shape[0]
    assert dec_seq == 1 and seq % CHUNK == 0 and seq % RWKV_TILE == 0 and nb % LANES == 0
    dn_alpha = (2 * depth) ** 0.25
    tm = 512
    tms = nb
    tiles_per_seq = seq // tm
    last_chunk = ((seq - 1) // CHUNK) * CHUNK

    xp = x_prompt.reshape(batch * seq, D_MODEL)
    xs = x_sample.reshape(nb, D_MODEL)
    pe_p = p_prompt.reshape(depth, batch * seq, PLE_DIM)
    pe_s = p_sample.reshape(depth, nb, PLE_DIM)
    lw = _prep_weights(p, nb)
    lw["ln_in_g"] = ln_in_g.reshape(1, D_MODEL)
    lw["ln_in_b"] = ln_in_b.reshape(1, D_MODEL)
    lw_p = dict(lw, wmix=lw["wmix_seq"], bmix=lw["bmix_seq"])
    lw_s = dict(lw, wmix=lw["wmix_one"], bmix=lw["bmix_one"])
    state_t = jnp.transpose(state_rwkv, (0, 2, 3, 4, 1))
    state_out = jnp.zeros_like(state_t)

    outs = [[] for _ in range(5)]
    for i in range(depth):
        first = i == 0
        res = _proj_call(xp, None, i, lw_p, tm, PROJ_SUB, tiles_per_seq, False, first)
        ga, gb, va, last = res[:4]
        yb, s_fin = _rwkv_seq_call(res[4:], i, lw, batch, seq)
        xp = _mix_call(xp, ga, gb, yb, pe_p, i, lw, tm, dn_alpha, first, False)
        outs[0].append(s_fin)
        outs[1].append(last[tiles_per_seq - 1::tiles_per_seq, 7, :])
        outs[2].append(va.reshape(batch, seq, D_A)[:, last_chunk:])

        res = _proj_call(xs, state_shift, i, lw_s, tms, tms, 1, True, first)
        ga, gb, va, pb = res[:4]
        yb_t, state_out = _rwkv_step_call(state_t, state_out, i, res[4:], lw, nb)
        xs = _mix_call(xs, ga, gb, yb_t, pe_s, i, lw, tms, dn_alpha, first, True)
        outs[3].append(pb)
        outs[4].append(va.reshape(nb, 1, D_A))

    return (xp.reshape(batch, seq, D_MODEL), xs.reshape(nb, 1, D_MODEL),
            jnp.stack(outs[0]), jnp.stack(outs[1]), jnp.stack(outs[2]),
            jnp.transpose(state_out, (0, 4, 1, 2, 3)), jnp.stack(outs[3]), jnp.stack(outs[4]))
```

```python
import functools

import jax
import jax.numpy as jnp
from jax import lax
from jax.experimental import pallas as pl
from jax.experimental.pallas import tpu as pltpu

F32 = jnp.float32
BF16 = jnp.bfloat16

D_MODEL = 1024
CHUNK = 128
D_A = 512
G_A = 4
D_B = 512
HEAD_B = 64
H_B = D_B // HEAD_B
LORA_W = 64
LORA_A = 64
LORA_G = 128
C_A = 2 * D_A
C_RWKV = 3 * D_B + LORA_W + LORA_A + LORA_G
C_GATE = 2 * D_MODEL
C_IN = C_A + C_RWKV + C_GATE
N_GROUPS = 4
EXPERTS_PER_GROUP = 4
N_EXPERTS = N_GROUPS * EXPERTS_PER_GROUP
D_EXPERT = 256
PLE_DIM = 256
LN_EPS = 1e-5
GN_EPS = 64e-5

LANES = 128
ROUTER_LANES = LANES
RWKV_CHUNK = 64
RWKV_TILE = 256
RWKV_TILES_PER_STEP = 2
PROJ_SUB = 128
VMEM_LIMIT = 56 * 1024 * 1024


def _ln(x, g, b, eps):
    mu = jnp.mean(x, axis=-1, keepdims=True)
    xc = x - mu
    var = jnp.mean(xc * xc, axis=-1, keepdims=True)
    return xc * lax.rsqrt(var + eps) * g + b


def _dot(a, b):
    return jnp.dot(a, b, preferred_element_type=F32)


def _split(x):
    hi = x.astype(BF16)
    lo = (x - hi.astype(F32)).astype(BF16)
    return hi, lo


def _dg(a, b, ca, cb):
    return lax.dot_general(a, b, (((ca,), (cb,)), ((), ())), preferred_element_type=F32)


def _gelu_tanh(x):
    return 0.5 * x * (1.0 + jnp.tanh(0.7978845608028654 * (x + 0.044715 * (x * x * x))))


def _sigmoid(x):
    return 1.0 / (1.0 + jnp.exp(-x))


def _softplus(x):
    return jnp.maximum(x, 0.0) + jnp.log(1.0 + jnp.exp(-jnp.abs(x)))


def _pair_head_sum(z):
    in_first = lax.broadcasted_iota(jnp.int32, (1, LANES), 1) < HEAD_B
    s0 = jnp.sum(jnp.where(in_first, z, 0.0), axis=-1, keepdims=True)
    s1 = jnp.sum(jnp.where(in_first, 0.0, z), axis=-1, keepdims=True)
    return jnp.where(in_first, s0, s1)


def _const_spec(shape):
    nd = len(shape)
    return pl.BlockSpec(shape, lambda *_: (0,) * nd, pipeline_mode=pl.Buffered(1))


def _layer_spec(arr, layer):
    nd = arr.ndim - 1
    return pl.BlockSpec((None,) + arr.shape[1:], lambda *_: (layer,) + (0,) * nd, pipeline_mode=pl.Buffered(1))


def _proj_kernel(*refs, tm, sub, tiles_per_seq, single_token, norm_input):
    if single_token:
        (x_ref, lning_ref, lninb_ref, prev_ref, win_ref, bgate_ref, lnvg_ref, lnvb_ref, wmix_ref, bmix_ref,
         waout_ref, mu_ref, w0_ref, ww2_ref, a0_ref, wa2_ref, wg2_ref, kkw_ref, kaw_ref,
         ga_ref, gb_ref, va_ref, last_ref, r_ref, lw_ref, k_ref, v_ref, kn_ref, ic_ref, g_ref) = refs
    else:
        (x_ref, lning_ref, lninb_ref, win_ref, bgate_ref, lnvg_ref, lnvb_ref, wmix_ref, bmix_ref,
         waout_ref, mu_ref, w0_ref, ww2_ref, a0_ref, wa2_ref, wg2_ref, kkw_ref, kaw_ref,
         ga_ref, gb_ref, va_ref, last_ref, r_ref, lw_ref, k_ref, v_ref, kn_ref, ic_ref, g_ref,
         carry_ref) = refs

    n_sub = tm // sub
    rr = lax.broadcasted_iota(jnp.int32, (CHUNK, CHUNK), 0)
    cc = lax.broadcasted_iota(jnp.int32, (CHUNK, CHUNK), 1)
    wmix = [jnp.where(rr >= cc, wmix_ref[g], 0.0).astype(BF16) for g in range(G_A)]
    if not single_token:
        i = pl.program_id(0)

        @pl.when(i == 0)
        def _():
            carry_ref[...] = jnp.zeros_like(carry_ref)

        first = jnp.where(i % tiles_per_seq == 0, jnp.zeros((1, C_RWKV), F32), carry_ref[0:1, :])
        row = lax.broadcasted_iota(jnp.int32, (sub, C_RWKV), 0)

    def project(q):
        x = x_ref[q * sub:(q + 1) * sub, :]
        if norm_input:
            x = _ln(x, lning_ref[...], lninb_ref[...], LN_EPS)
        xb = x.astype(BF16)
        return (_dot(xb, win_ref[:, 0:C_A]), _dot(xb, win_ref[:, C_A + C_RWKV:]),
                _dot(xb, win_ref[:, C_A:C_A + C_RWKV]))

    def finish(q, pa, pg, pb, prev_row):
        rows = slice(q * sub, (q + 1) * sub)
        z = _gelu_tanh(pa)
        u = z[:, :D_A]
        va = _ln(z[:, D_A:], lnvg_ref[...], lnvb_ref[...], LN_EPS)
        va_ref[rows, :] = va
        vab = va.astype(BF16)
        mix = jnp.concatenate(
            [jnp.concatenate([_dot(wmix[g], vab[c * CHUNK:(c + 1) * CHUNK, g * LANES:(g + 1) * LANES])
                              for g in range(G_A)], axis=1) + bmix_ref[...]
             for c in range(sub // CHUNK)], axis=0)
        ao = _dot((u * mix).astype(BF16), waout_ref[...])
        gates = _sigmoid(pg + bgate_ref[...])
        ga_ref[rows, :] = (gates[:, :D_MODEL] * ao).astype(BF16)
        gb_ref[rows, :] = gates[:, D_MODEL:].astype(BF16)

        if single_token:
            prev = prev_ref[rows, :]
        else:
            prev = jnp.where(row == 0, prev_row, pltpu.roll(pb, 1, 0))
        xs = pb + mu_ref[...] * (prev - pb)
        r = xs[:, 0:D_B]
        k = xs[:, D_B:2 * D_B]
        v = xs[:, 2 * D_B:3 * D_B]
        o = 3 * D_B
        wa_d = xs[:, o:o + LORA_W + LORA_A]
        gd = xs[:, o + LORA_W + LORA_A:]
        w_log = -_softplus(-(w0_ref[...] + _dot(jnp.tanh(wa_d).astype(BF16), ww2_ref[...]))) - 0.5
        iclr = _sigmoid(a0_ref[...] + _dot(wa_d.astype(BF16), wa2_ref[...]))
        kk = k * kkw_ref[...]
        sq = kk * kk
        ss = jnp.concatenate([_pair_head_sum(sq[:, p * LANES:(p + 1) * LANES]) for p in range(D_B // LANES)], axis=1)
        outs = ((lw_ref, -jnp.exp(w_log)), (g_ref, _dot(_sigmoid(gd).astype(BF16), wg2_ref[...])),
                (kn_ref, kk / jnp.maximum(jnp.sqrt(ss), 1e-12)), (k_ref, k * (1.0 + (iclr - 1.0) * kaw_ref[...])),
                (r_ref, r), (v_ref, v), (ic_ref, iclr))
        if single_token:
            t = jnp.concatenate([val for _, val in outs], axis=1).T
            for j, (ref, _) in enumerate(outs):
                ref[:, rows] = t[j * D_B:(j + 1) * D_B, :]
        else:
            for ref, val in outs:
                ref[rows, :] = val

    proj = [project(0)]
    for q in range(n_sub):
        if q + 1 < n_sub:
            proj.append(project(q + 1))
        prev_row = None
        if not single_token:
            prev_row = first if q == 0 else proj[q - 1][2][sub - 1:sub, :]
        finish(q, *proj[q], prev_row)
    pb_last = proj[n_sub - 1][2]
    if single_token:
        last_ref[...] = pb_last
    else:
        carry_ref[0:1, :] = pb_last[sub - 1:sub, :]
        last_ref[0] = pb_last[sub - 8:sub, :]


def _proj_call(x, prev, layer, lw, tm, sub, tiles_per_seq, single_token, norm_input):
    m = x.shape[0]
    nt = m // tm
    row = lambda n: pl.BlockSpec((tm, n), lambda i: (i, 0))
    in_specs = [row(D_MODEL), _const_spec((1, D_MODEL)), _const_spec((1, D_MODEL))]
    args = [x, lw["ln_in_g"], lw["ln_in_b"]]
    if single_token:
        in_specs.append(pl.BlockSpec((None, tm, C_RWKV), lambda i: (layer, i, 0)))
        args.append(prev)
    weights = [lw[n] for n in ("w_in", "b_gate", "ln_v_g", "ln_v_b", "wmix", "bmix", "w_a_out", "mu_shift", "w0",
                               "w_w2p", "a0", "w_a2p", "w_g2", "k_k", "k_a")]
    in_specs += [_layer_spec(w, layer) for w in weights]
    args += weights
    out_shape = [jax.ShapeDtypeStruct((m, D_MODEL), BF16), jax.ShapeDtypeStruct((m, D_MODEL), BF16),
                 jax.ShapeDtypeStruct((m, D_A), F32)]
    out_specs = [row(D_MODEL), row(D_MODEL), row(D_A)]
    if single_token:
        out_shape.append(jax.ShapeDtypeStruct((m, C_RWKV), F32))
        out_specs.append(row(C_RWKV))
    else:
        out_shape.append(jax.ShapeDtypeStruct((nt, 8, C_RWKV), F32))
        out_specs.append(pl.BlockSpec((1, 8, C_RWKV), lambda i: (i, 0, 0)))
    if single_token:
        out_shape += [jax.ShapeDtypeStruct((D_B, m), F32)] * 7
        out_specs += [pl.BlockSpec((D_B, tm), lambda i: (0, i))] * 7
    else:
        out_shape += [jax.ShapeDtypeStruct((m, D_B), F32)] * 7
        out_specs += [row(D_B)] * 7
    scratch = [] if single_token else [pltpu.VMEM((8, C_RWKV), F32)]
    return pl.pallas_call(
        functools.partial(_proj_kernel, tm=tm, sub=sub, tiles_per_seq=tiles_per_seq, single_token=single_token,
                          norm_input=norm_input),
        grid=(nt,),
        in_specs=in_specs,
        out_specs=out_specs,
        out_shape=out_shape,
        scratch_shapes=scratch,
        compiler_params=pltpu.CompilerParams(dimension_semantics=("arbitrary",), vmem_limit_bytes=VMEM_LIMIT),
        name="proj_single" if single_token else "proj_seq",
    )(*args)


def _rwkv_tile_kernel(r_ref, lw_ref, k_ref, v_ref, kn_ref, ic_ref, g_ref, lnxg_ref, lnxb_ref, rkw_ref,
                      y_ref, sfin_ref, state_ref, *, n_tile):
    ti = pl.program_id(1)
    n_t = pl.num_programs(1)
    C = RWKV_CHUNK
    RT = RWKV_TILE
    n_ch = RT // C
    n_pair = H_B // 2

    @pl.when(ti == 0)
    def _():
        state_ref[...] = jnp.zeros_like(state_ref)

    row = lax.broadcasted_iota(jnp.int32, (RT, RT), 0)
    col = lax.broadcasted_iota(jnp.int32, (RT, RT), 1)
    same = (row // C) == (col // C)
    incl = jnp.logical_and(same, row >= col)
    strict = jnp.logical_and(same, row > col)
    row_in_chunk = lax.broadcasted_iota(jnp.int32, (RT, 1), 0) % C
    eye = jnp.where(row == col, 1.0, 0.0)
    prow = lax.broadcasted_iota(jnp.int32, (LANES, LANES), 0)
    pcol = lax.broadcasted_iota(jnp.int32, (LANES, LANES), 1)
    pair_bd = (prow // HEAD_B) == (pcol // HEAD_B)
    in_first = lax.broadcasted_iota(jnp.int32, (1, LANES), 1) < HEAD_B
    first_head = jnp.where(in_first, 1.0, 0.0)
    second_head = 1.0 - first_head
    sls = [slice(p * LANES, (p + 1) * LANES) for p in range(n_pair)]
    pr = [h // 2 for h in range(H_B)]

    def prologue(t):
        rows = slice(t * RT, (t + 1) * RT)
        lw = lw_ref[rows, :]
        cum = lw
        for shift in (1, 2, 4, 8, 16, 32):
            cum = cum + jnp.where(row_in_chunk >= shift, pltpu.roll(cum, shift, 0), 0.0)
        ends = [cum[(c + 1) * C - 1:(c + 1) * C, :] for c in range(n_ch)]
        cum_end = jnp.concatenate([jnp.broadcast_to(e, (C, D_B)) for e in ends], axis=0)
        d = dict(rows=rows, ends=ends, atm=[], rtm=[], rt=[], bt=[], kt=[], bh=[], kh=[], v=[])
        for p in range(n_pair):
            sl = sls[p]
            cum_p = cum[:, sl]
            kn_p = kn_ref[rows, sl]
            b_p = kn_p * ic_ref[rows, sl]
            k_p = k_ref[rows, sl]
            e_inv = jnp.exp(-cum_p)
            e_hat = jnp.exp(cum_end[:, sl] - cum_p)
            at_p = -kn_p * jnp.exp(cum_p - lw[:, sl])
            rt_p = r_ref[rows, sl] * jnp.exp(cum_p)
            for hm in (first_head, second_head):
                d["atm"].append((at_p * hm).astype(BF16))
                d["rtm"].append((rt_p * hm).astype(BF16))
            d["rt"].append(rt_p)
            d["bt"].append((b_p * e_inv).astype(BF16))
            d["kt"].append((k_p * e_inv).astype(BF16))
            d["bh"].append((b_p * e_hat).astype(BF16))
            d["kh"].append((k_p * e_hat).astype(BF16))
            d["v"].append(v_ref[rows, sl].astype(BF16))
        return d

    def main(d):
        atm, rtm, bt, kt, v = d["atm"], d["rtm"], d["bt"], d["kt"], d["v"]
        xs = [jnp.where(strict, _dg(atm[h], bt[pr[h]], 1, 1), 0.0) for h in range(H_B)]
        yield
        ts = [eye + x for x in xs]
        xb = [x.astype(BF16) for x in xs]
        for _ in range(5):
            xb = [_dot(z, z).astype(BF16) for z in xb]
            yield
            ts = [t + _dot(t.astype(BF16), z) for t, z in zip(ts, xb)]
            yield
        tb = [t.astype(BF16) for t in ts]
        a_ak = [jnp.where(strict, _dg(atm[h], kt[pr[h]], 1, 1), 0.0).astype(BF16) for h in range(H_B)]
        yield
        av = [_dot(a_ak[h], v[pr[h]]).astype(BF16) for h in range(H_B)]
        yield
        tx = [_dot(tb[h], jnp.concatenate([av[h], atm[h]], axis=1)) for h in range(H_B)]
        yield
        a_rb = [jnp.where(incl, _dg(rtm[h], bt[pr[h]], 1, 1), 0.0).astype(BF16) for h in range(H_B)]
        yield
        a_rk = [jnp.where(incl, _dg(rtm[h], kt[pr[h]], 1, 1), 0.0).astype(BF16) for h in range(H_B)]
        yield
        rb = [_dot(a_rb[h], tx[h].astype(BF16)) for h in range(H_B)]
        yield
        rkv = [_dot(a_rk[h], v[pr[h]]) for h in range(H_B)]
        d["w2"] = [jnp.where(in_first, tx[2 * p][:, :LANES], tx[2 * p + 1][:, :LANES]).astype(BF16)
                   for p in range(n_pair)]
        d["a_t"] = [(tx[2 * p][:, LANES:] + tx[2 * p + 1][:, LANES:]).astype(BF16) for p in range(n_pair)]
        d["rhat"] = [(d["rt"][p] + (rb[2 * p][:, LANES:] + rb[2 * p + 1][:, LANES:])).astype(BF16)
                     for p in range(n_pair)]
        d["y0"] = [jnp.where(in_first, rb[2 * p][:, :LANES] + rkv[2 * p], rb[2 * p + 1][:, :LANES] + rkv[2 * p + 1])
                   for p in range(n_pair)]
        yield

    def tail(d, st):
        w2, a_t, rhat, y0, bh, kh, v = d["w2"], d["a_t"], d["rhat"], d["y0"], d["bh"], d["kh"], d["v"]
        ys = [[] for _ in range(n_pair)]
        for c in range(n_ch):
            rows = slice(c * C, (c + 1) * C)
            el = [jnp.where(pair_bd, _dg(a_t[p][rows], bh[p][rows], 0, 0), 0.0).astype(BF16) for p in range(n_pair)]
            gm = [jnp.where(pair_bd,
                            _dg(jnp.concatenate([w2[p][rows], v[p][rows]], axis=0),
                                jnp.concatenate([bh[p][rows], kh[p][rows]], axis=0), 0, 0), 0.0)
                  for p in range(n_pair)]
            s = st["s"]
            sb = [z.astype(BF16) for z in s]
            for p in range(n_pair):
                ys[p].append(_dg(rhat[p][rows], sb[p], 1, 1) + y0[p][rows])
            st["s"] = [s[p] * jnp.exp(d["ends"][c][:, sls[p]]) + _dot(sb[p], el[p]) + gm[p] for p in range(n_pair)]
            yield
        rows = d["rows"]
        for p in range(n_pair):
            sl = sls[p]
            y = jnp.concatenate(ys[p], axis=0)
            mu = _pair_head_sum(y) * (1.0 / HEAD_B)
            yc = y - mu
            var = _pair_head_sum(yc * yc) * (1.0 / HEAD_B)
            yn = yc * lax.rsqrt(var + GN_EPS) * lnxg_ref[:, sl] + lnxb_ref[:, sl]
            bonus = _pair_head_sum(r_ref[rows, sl] * k_ref[rows, sl] * rkw_ref[:, sl]) * v_ref[rows, sl]
            y_ref[rows, sl] = (yn + bonus) * g_ref[rows, sl]
            yield

    def drain(gen):
        for _ in gen:
            pass

    st = {"s": [state_ref[p] for p in range(n_pair)]}
    pro = prologue(0)
    stages = main(pro)
    next(stages)
    prev_tail = None
    for t in range(n_tile):
        nxt = prologue(t + 1) if t + 1 < n_tile else None
        for n, _ in enumerate(stages):
            if prev_tail is not None and n % 2 == 1:
                next(prev_tail, None)
        if prev_tail is not None:
            drain(prev_tail)
        prev_tail = tail(pro, st)
        if nxt is not None:
            pro = nxt
            stages = main(pro)
            next(stages)
    drain(prev_tail)
    for p in range(n_pair):
        state_ref[p] = st["s"][p]

    @pl.when(ti == n_t - 1)
    def _():
        sfin_ref[0] = state_ref[...]


def _rwkv_seq_call(ops, layer, lw, batch, seq):
    RT = RWKV_TILE * RWKV_TILES_PER_STEP
    n_t = seq // RT
    n_pair = H_B // 2
    blk = pl.BlockSpec((RT, D_B), lambda b, t: (b * n_t + t, 0))
    vec = pl.BlockSpec((None, 1, D_B), lambda b, t: (layer, 0, 0))
    y, s_bd = pl.pallas_call(
        functools.partial(_rwkv_tile_kernel, n_tile=RWKV_TILES_PER_STEP),
        grid=(batch, n_t),
        in_specs=[blk] * 7 + [vec] * 3,
        out_specs=[blk, pl.BlockSpec((1, n_pair, LANES, LANES), lambda b, t: (b, 0, 0, 0))],
        out_shape=[jax.ShapeDtypeStruct((batch * seq, D_B), F32),
                   jax.ShapeDtypeStruct((batch, n_pair, LANES, LANES), F32)],
        scratch_shapes=[pltpu.VMEM((n_pair, LANES, LANES), F32)],
        compiler_params=pltpu.CompilerParams(dimension_semantics=("arbitrary", "arbitrary"),
                                             vmem_limit_bytes=VMEM_LIMIT),
        name="rwkv_seq",
    )(*ops, lw["lnx_g"], lw["lnx_b"], lw["r_k"])
    s_bd = s_bd.reshape(batch, n_pair, 2, HEAD_B, 2, HEAD_B)
    s_fin = jnp.stack([s_bd[:, :, 0, :, 0, :], s_bd[:, :, 1, :, 1, :]], axis=2)
    return y, s_fin.reshape(batch, H_B, HEAD_B, HEAD_B)


def _rwkv_step_kernel(s_ref, r_ref, lw_ref, k_ref, v_ref, kn_ref, ic_ref, g_ref, lnxg_ref, lnxb_ref, rkw_ref,
                      buf_ref, y_ref, so_ref, y_scr):
    del buf_ref
    kn = kn_ref[...]
    a = -kn
    b = kn * ic_ref[...]
    w = jnp.exp(lw_ref[...])
    k = k_ref[...]
    r = r_ref[...]
    for i in range(HEAD_B):
        s = s_ref[i]
        sa = jnp.sum(s * a, axis=0, keepdims=True)
        s1 = s * w + sa * b + v_ref[i:i + 1, :] * k
        so_ref[i] = s1
        y_scr[i:i + 1, :] = jnp.sum(s1 * r, axis=0, keepdims=True)
    y = y_scr[...]
    mu = jnp.mean(y, axis=0, keepdims=True)
    yc = y - mu
    var = jnp.mean(yc * yc, axis=0, keepdims=True)
    yn = yc * lax.rsqrt(var + GN_EPS) * lnxg_ref[...] + lnxb_ref[...]
    bonus = jnp.sum(r * k * rkw_ref[...], axis=0, keepdims=True) * v_ref[...]
    y_ref[...] = (yn + bonus) * g_ref[...]


def _rwkv_step_call(state_t, buf, layer, ops_t, lw, nb):
    sblk = pl.BlockSpec((None, None, HEAD_B, HEAD_B, nb), lambda h: (layer, h, 0, 0, 0))
    vblk = pl.BlockSpec((HEAD_B, nb), lambda h: (h, 0))
    pblk = pl.BlockSpec((None, HEAD_B, nb), lambda h: (layer, h, 0))
    return pl.pallas_call(
        _rwkv_step_kernel,
        grid=(H_B,),
        in_specs=[sblk] + [vblk] * 7 + [pblk] * 3 + [pl.BlockSpec(memory_space=pl.ANY)],
        out_specs=[vblk, sblk],
        out_shape=[jax.ShapeDtypeStruct((D_B, nb), F32), jax.ShapeDtypeStruct(buf.shape, F32)],
        scratch_shapes=[pltpu.VMEM((HEAD_B, nb), F32)],
        input_output_aliases={11: 1},
        compiler_params=pltpu.CompilerParams(dimension_semantics=("arbitrary",)),
        name="rwkv_step",
    )(state_t, *ops_t, lw["lnx_g_t"], lw["lnx_b_t"], lw["r_k_t"], buf)


def _route(lg, tm):
    lane = lax.broadcasted_iota(jnp.int32, (tm, ROUTER_LANES), 1)
    lanef = lane.astype(F32)
    big = jnp.float32(1e9)
    ninf = jnp.float32(-jnp.inf)
    isg = lane < N_GROUPS
    lgm = jnp.where(isg, lg, ninf)
    gmax = jnp.max(lgm, axis=-1, keepdims=True)
    den = jnp.sum(jnp.exp(lgm - gmax), axis=-1, keepdims=True)
    gp = 1.0 / den
    gi = jnp.min(jnp.where(lgm == gmax, lanef, big), axis=-1, keepdims=True)
    grp = ((lane - N_GROUPS) >> 2).astype(F32)
    ise = jnp.where(lane >= N_GROUPS, grp, big) == gi
    le = jnp.where(ise, lg, ninf)
    m1 = jnp.max(le, axis=-1, keepdims=True)
    i1 = jnp.min(jnp.where(le == m1, lanef, big), axis=-1, keepdims=True)
    le2 = jnp.where(lanef == i1, ninf, le)
    m2 = jnp.max(le2, axis=-1, keepdims=True)
    i2 = jnp.min(jnp.where(le2 == m2, lanef, big), axis=-1, keepdims=True)
    e2 = jnp.exp(m2 - m1)
    w1 = 1.0 / (1.0 + e2)
    w2 = e2 / (1.0 + e2)
    return jnp.where(lanef == i1, w1 * gp, 0.0) + jnp.where(lanef == i2, w2 * gp, 0.0)


def _mix_kernel(x_ref, lning_ref, lninb_ref, ga_ref, gb_ref, yb_ref, pe_ref, wbout_ref, wo_ref, ln1g_ref,
                ln1b_ref, wr_ref, br_ref, weg_ref, weu_ref, wed_ref, wpg_ref, wpp_ref, ln2g_ref, ln2b_ref,
                o_ref, *, tm, dn_alpha, norm_input, yb_on_lanes):
    x = x_ref[...]
    if norm_input:
        x = _ln(x, lning_ref[...], lninb_ref[...], LN_EPS)
    if yb_on_lanes:
        bo = _dg(yb_ref[...].astype(BF16), wbout_ref[...], 0, 0)
    else:
        bo = _dot(yb_ref[...].astype(BF16), wbout_ref[...])
    merged = ga_ref[...] + gb_ref[...] * bo
    mix = _dot(merged.astype(BF16), wo_ref[...])
    x1 = _ln(dn_alpha * x + mix, ln1g_ref[...], ln1b_ref[...], LN_EPS)
    x1b = x1.astype(BF16)

    x1h, x1l = _split(x1)
    wrh = wr_ref[0]
    wrl = wr_ref[1]
    lg = _dot(x1h, wrh) + (_dot(x1h, wrl) + _dot(x1l, wrh)) + br_ref[...]
    comb = _route(lg, tm)

    moe = jnp.zeros((tm, D_MODEL), F32)
    for grp in range(N_GROUPS):
        hs = []
        for j in range(EXPERTS_PER_GROUP):
            e = grp * EXPERTS_PER_GROUP + j
            hg = _dot(x1b, weg_ref[e])
            hu = _dot(x1b, weu_ref[e])
            ce = comb[:, N_GROUPS + e:N_GROUPS + e + 1]
            hs.append(((hg * _sigmoid(hg)) * hu * ce).astype(BF16))
        hcat = jnp.concatenate(hs, axis=1)
        rows = slice(grp * EXPERTS_PER_GROUP * D_EXPERT, (grp + 1) * EXPERTS_PER_GROUP * D_EXPERT)
        moe = moe + _dot(hcat, wed_ref[rows, :])

    ple = _sigmoid(_dot(x1b, wpg_ref[...])) * _dot(pe_ref[...].astype(BF16), wpp_ref[...])
    o_ref[...] = _ln(dn_alpha * x1 + moe + ple, ln2g_ref[...], ln2b_ref[...], LN_EPS)


def _mix_call(x, ga, gb, yb, pe, layer, lw, tm, dn_alpha, norm_input, yb_on_lanes):
    m = x.shape[0]
    row = lambda n: pl.BlockSpec((tm, n), lambda i: (i, 0))
    yb_spec = pl.BlockSpec((D_B, tm), lambda i: (0, i)) if yb_on_lanes else row(D_B)
    weights = [lw[n] for n in ("w_b_out", "w_o", "ln1_g", "ln1_b", "w_router", "b_router", "w_e_gate", "w_e_up",
                               "w_e_down", "w_pe_gate", "w_pe_proj", "ln2_g", "ln2_b")]
    return pl.pallas_call(
        functools.partial(_mix_kernel, tm=tm, dn_alpha=dn_alpha, norm_input=norm_input, yb_on_lanes=yb_on_lanes),
        grid=(m // tm,),
        in_specs=[row(D_MODEL), _const_spec((1, D_MODEL)), _const_spec((1, D_MODEL)), row(D_MODEL), row(D_MODEL),
                  yb_spec, pl.BlockSpec((None, tm, PLE_DIM), lambda i: (layer, i, 0))]
                 + [_layer_spec(w, layer) for w in weights],
        out_specs=row(D_MODEL),
        out_shape=jax.ShapeDtypeStruct((m, D_MODEL), F32),
        compiler_params=pltpu.CompilerParams(dimension_semantics=("parallel",), vmem_limit_bytes=VMEM_LIMIT),
        name="mix_moe",
    )(x, lw["ln_in_g"], lw["ln_in_b"], ga, gb, yb, pe, *weights)


def _prep_weights(p, nb):
    depth = p["w_in"].shape[0]
    vec = lambda z: z.reshape(depth, 1, -1)
    bf = lambda z: z.astype(BF16)
    lanes = lambda z: jnp.broadcast_to(z.reshape(depth, D_B, 1), (depth, D_B, nb))
    zeros64 = jnp.zeros((depth, LORA_W, D_B), F32)
    pad = ROUTER_LANES - N_GROUPS - N_EXPERTS
    w_router = jnp.concatenate(
        [p["w_rg"], jnp.transpose(p["w_re"], (0, 2, 1, 3)).reshape(depth, D_MODEL, N_EXPERTS),
         jnp.zeros((depth, D_MODEL, pad), F32)], axis=2)
    wr_hi = w_router.astype(BF16)
    wr_lo = (w_router - wr_hi.astype(F32)).astype(BF16)
    b_router = jnp.concatenate([p["b_rg"], p["b_re"].reshape(depth, N_EXPERTS), jnp.zeros((depth, pad), F32)], axis=1)
    eye = jnp.eye(CHUNK, dtype=F32)
    w = {n: vec(p[n]) for n in ("b_gate", "ln_v_g", "ln_v_b", "mu_shift", "w0", "a0", "k_k", "k_a", "lnx_g", "lnx_b",
                                "ln1_g", "ln1_b", "ln2_g", "ln2_b")}
    w.update({n: bf(p[n]) for n in ("w_in", "w_a_out", "w_g2", "w_b_out", "w_o", "w_e_gate", "w_e_up", "w_pe_gate",
                                    "w_pe_proj")})
    w.update({
        "w_w2p": bf(jnp.concatenate([p["w_w2"], zeros64], axis=1)),
        "w_a2p": bf(jnp.concatenate([zeros64, p["w_a2"]], axis=1)),
        "r_k": vec(p["r_k"]),
        "lnx_g_t": lanes(p["lnx_g"]), "lnx_b_t": lanes(p["lnx_b"]), "r_k_t": lanes(p["r_k"]),
        "w_router": jnp.stack([wr_hi, wr_lo], axis=1), "b_router": vec(b_router),
        "w_e_down": bf(p["w_e_down"]).reshape(depth, N_EXPERTS * D_EXPERT, D_MODEL),
        "wmix_seq": p["w_s"],
        "bmix_seq": jnp.repeat(jnp.transpose(p["b_s"], (0, 2, 1)), LANES, axis=2),
        "wmix_one": p["w_s"][:, :, 0, 0][:, :, None, None] * eye,
        "bmix_one": jnp.broadcast_to(jnp.repeat(p["b_s"][:, :, 0], LANES, axis=1)[:, None, :], (depth, CHUNK, D_A)),
    })
    return w


def kernel(x_prompt, x_sample, state_rwkv, state_shift, p_prompt, p_sample, ln_in_g, ln_in_b, w_in, b_gate,
           ln_v_g, ln_v_b, w_s, b_s, w_a_out, mu_shift, w0, w_w2, a0, w_a2, w_g2, k_k, k_a, r_k, lnx_g, lnx_b,
           w_b_out, w_o, ln1_g, ln1_b, w_rg, b_rg, w_re, b_re, w_e_gate, w_e_up, w_e_down, w_pe_gate,
           w_pe_proj, ln2_g, ln2_b):
    p = dict(w_in=w_in, b_gate=b_gate, ln_v_g=ln_v_g, ln_v_b=ln_v_b, w_s=w_s, b_s=b_s, w_a_out=w_a_out,
             mu_shift=mu_shift, w0=w0, w_w2=w_w2, a0=a0, w_a2=w_a2, w_g2=w_g2, k_k=k_k, k_a=k_a, r_k=r_k,
             lnx_g=lnx_g, lnx_b=lnx_b, w_b_out=w_b_out, w_o=w_o, ln1_g=ln1_g, ln1_b=ln1_b, w_rg=w_rg, b_rg=b_rg,
             w_re=w_re, b_re=b_re, w_e_gate=w_e_gate, w_e_up=w_e_up, w_e_down=w_e_down,
             w_pe_gate=w_pe_gate, w_pe_proj=w_pe_proj, ln2_g=ln2_g, ln2_b=ln2_b)
    batch, seq, _ = x_prompt.shape
    nb, dec_seq, _ = x_sample.shape
    depth = w_in.shape[0]
    assert dec_seq == 1 and seq % CHUNK == 0 and seq % (RWKV_TILE * RWKV_TILES_PER_STEP) == 0 and nb % LANES == 0
    dn_alpha = (2 * depth) ** 0.25
    tm = 512
    tms = nb
    tiles_per_seq = seq // tm
    last_chunk = ((seq - 1) // CHUNK) * CHUNK

    xp = x_prompt.reshape(batch * seq, D_MODEL)
    xs = x_sample.reshape(nb, D_MODEL)
    pe_p = p_prompt.reshape(depth, batch * seq, PLE_DIM)
    pe_s = p_sample.reshape(depth, nb, PLE_DIM)
    lw = _prep_weights(p, nb)
    lw["ln_in_g"] = ln_in_g.reshape(1, D_MODEL)
    lw["ln_in_b"] = ln_in_b.reshape(1, D_MODEL)
    lw_p = dict(lw, wmix=lw["wmix_seq"], bmix=lw["bmix_seq"])
    lw_s = dict(lw, wmix=lw["wmix_one"], bmix=lw["bmix_one"])
    state_t = jnp.transpose(state_rwkv, (0, 2, 3, 4, 1))
    state_out = jnp.zeros_like(state_t)

    outs = [[] for _ in range(5)]
    for i in range(depth):
        first = i == 0
        res = _proj_call(xp, None, i, lw_p, tm, PROJ_SUB, tiles_per_seq, False, first)
        ga, gb, va, last = res[:4]
        yb, s_fin = _rwkv_seq_call(res[4:], i, lw, batch, seq)
        xp = _mix_call(xp, ga, gb, yb, pe_p, i, lw, tm, dn_alpha, first, False)
        outs[0].append(s_fin)
        outs[1].append(last[tiles_per_seq - 1::tiles_per_seq, 7, :])
        outs[2].append(va.reshape(batch, seq, D_A)[:, last_chunk:])

        res = _proj_call(xs, state_shift, i, lw_s, tms, tms, 1, True, first)
        ga, gb, va, pb = res[:4]
        yb_t, state_out = _rwkv_step_call(state_t, state_out, i, res[4:], lw, nb)
        xs = _mix_call(xs, ga, gb, yb_t, pe_s, i, lw, tms, dn_alpha, first, True)
        outs[3].append(pb)
        outs[4].append(va.reshape(nb, 1, D_A))

    return (xp.reshape(batch, seq, D_MODEL), xs.reshape(nb, 1, D_MODEL),
            jnp.stack(outs[0]), jnp.stack(outs[1]), jnp.stack(outs[2]),
            jnp.transpose(state_out, (0, 4, 1, 2, 3)), jnp.stack(outs[3]), jnp.stack(outs[4]))
```

```python
import functools

import jax
import jax.numpy as jnp
from jax import lax
from jax.experimental import pallas as pl
from jax.experimental.pallas import tpu as pltpu

F32 = jnp.float32
BF16 = jnp.bfloat16

D_MODEL = 1024
CHUNK = 128
D_A = 512
G_A = 4
D_B = 512
HEAD_B = 64
H_B = D_B // HEAD_B
LORA_W = 64
LORA_A = 64
LORA_G = 128
C_A = 2 * D_A
C_RWKV = 3 * D_B + LORA_W + LORA_A + LORA_G
C_GATE = 2 * D_MODEL
C_IN = C_A + C_RWKV + C_GATE
N_GROUPS = 4
EXPERTS_PER_GROUP = 4
N_EXPERTS = N_GROUPS * EXPERTS_PER_GROUP
D_EXPERT = 256
PLE_DIM = 256
LN_EPS = 1e-5
GN_EPS = 64e-5

LANES = 128
ROUTER_LANES = LANES
RWKV_CHUNK = 64
RWKV_TILE = 256
RWKV_TILES_PER_STEP = 2
OP_R, OP_LW, OP_K, OP_V, OP_KN, OP_IC, OP_G = range(7)
N_OPS = 7
MOE_CAP = 256
PROJ_SUB = 128
VMEM_LIMIT = 56 * 1024 * 1024


def _ln(x, g, b, eps):
    mu = jnp.mean(x, axis=-1, keepdims=True)
    xc = x - mu
    var = jnp.mean(xc * xc, axis=-1, keepdims=True)
    return xc * lax.rsqrt(var + eps) * g + b


def _dot(a, b):
    return jnp.dot(a, b, preferred_element_type=F32)


def _split(x):
    hi = x.astype(BF16)
    lo = (x - hi.astype(F32)).astype(BF16)
    return hi, lo


def _dg(a, b, ca, cb):
    return lax.dot_general(a, b, (((ca,), (cb,)), ((), ())), preferred_element_type=F32)


def _gelu_tanh(x):
    return 0.5 * x * (1.0 + jnp.tanh(0.7978845608028654 * (x + 0.044715 * (x * x * x))))


def _sigmoid(x):
    return 1.0 / (1.0 + jnp.exp(-x))


def _softplus(x):
    return jnp.maximum(x, 0.0) + jnp.log(1.0 + jnp.exp(-jnp.abs(x)))


def _pair_head_sum(z):
    in_first = lax.broadcasted_iota(jnp.int32, (1, LANES), 1) < HEAD_B
    s0 = jnp.sum(jnp.where(in_first, z, 0.0), axis=-1, keepdims=True)
    s1 = jnp.sum(jnp.where(in_first, 0.0, z), axis=-1, keepdims=True)
    return jnp.where(in_first, s0, s1)


def _const_spec(shape):
    nd = len(shape)
    return pl.BlockSpec(shape, lambda *_: (0,) * nd, pipeline_mode=pl.Buffered(1))


def _layer_spec(arr, layer):
    nd = arr.ndim - 1
    return pl.BlockSpec((None,) + arr.shape[1:], lambda *_: (layer,) + (0,) * nd, pipeline_mode=pl.Buffered(1))


def _proj_kernel(*refs, tm, sub, tiles_per_seq, single_token, norm_input):
    if single_token:
        (x_ref, lning_ref, lninb_ref, prev_ref, win_ref, bgate_ref, lnvg_ref, lnvb_ref, wmix_ref, bmix_ref,
         waout_ref, mu_ref, w0_ref, ww2_ref, a0_ref, wa2_ref, wg2_ref, kkw_ref, kaw_ref,
         ga_ref, gb_ref, va_ref, last_ref, ops_ref) = refs
    else:
        (x_ref, lning_ref, lninb_ref, win_ref, bgate_ref, lnvg_ref, lnvb_ref, wmix_ref, bmix_ref,
         waout_ref, mu_ref, w0_ref, ww2_ref, a0_ref, wa2_ref, wg2_ref, kkw_ref, kaw_ref,
         ga_ref, gb_ref, va_ref, last_ref, ops_ref, carry_ref) = refs

    n_sub = tm // sub
    rr = lax.broadcasted_iota(jnp.int32, (CHUNK, CHUNK), 0)
    cc = lax.broadcasted_iota(jnp.int32, (CHUNK, CHUNK), 1)
    wmix = [jnp.where(rr >= cc, wmix_ref[g], 0.0).astype(BF16) for g in range(G_A)]
    if not single_token:
        i = pl.program_id(0)

        @pl.when(i == 0)
        def _():
            carry_ref[...] = jnp.zeros_like(carry_ref)

        first = jnp.where(i % tiles_per_seq == 0, jnp.zeros((1, C_RWKV), F32), carry_ref[0:1, :])
        row = lax.broadcasted_iota(jnp.int32, (sub, C_RWKV), 0)

    def project(q):
        x = x_ref[q * sub:(q + 1) * sub, :]
        if norm_input:
            x = _ln(x, lning_ref[...], lninb_ref[...], LN_EPS)
        xb = x.astype(BF16)
        return (_dot(xb, win_ref[:, 0:C_A]), _dot(xb, win_ref[:, C_A + C_RWKV:]),
                _dot(xb, win_ref[:, C_A:C_A + C_RWKV]))

    def finish(q, pa, pg, pb, prev_row):
        rows = slice(q * sub, (q + 1) * sub)
        z = _gelu_tanh(pa)
        u = z[:, :D_A]
        va = _ln(z[:, D_A:], lnvg_ref[...], lnvb_ref[...], LN_EPS)
        va_ref[rows, :] = va
        vab = va.astype(BF16)
        mix = jnp.concatenate(
            [jnp.concatenate([_dot(wmix[g], vab[c * CHUNK:(c + 1) * CHUNK, g * LANES:(g + 1) * LANES])
                              for g in range(G_A)], axis=1) + bmix_ref[...]
             for c in range(sub // CHUNK)], axis=0)
        ao = _dot((u * mix).astype(BF16), waout_ref[...])
        gates = _sigmoid(pg + bgate_ref[...])
        ga_ref[rows, :] = (gates[:, :D_MODEL] * ao).astype(BF16)
        gb_ref[rows, :] = gates[:, D_MODEL:].astype(BF16)

        if single_token:
            prev = prev_ref[rows, :]
        else:
            prev = jnp.where(row == 0, prev_row, pltpu.roll(pb, 1, 0))
        xs = pb + mu_ref[...] * (prev - pb)
        r = xs[:, 0:D_B]
        k = xs[:, D_B:2 * D_B]
        v = xs[:, 2 * D_B:3 * D_B]
        o = 3 * D_B
        wa_d = xs[:, o:o + LORA_W + LORA_A]
        gd = xs[:, o + LORA_W + LORA_A:]
        w_log = -_softplus(-(w0_ref[...] + _dot(jnp.tanh(wa_d).astype(BF16), ww2_ref[...]))) - 0.5
        iclr = _sigmoid(a0_ref[...] + _dot(wa_d.astype(BF16), wa2_ref[...]))
        kk = k * kkw_ref[...]
        sq = kk * kk
        ss = jnp.concatenate([_pair_head_sum(sq[:, p * LANES:(p + 1) * LANES]) for p in range(D_B // LANES)], axis=1)
        packed = jnp.concatenate(
            [r, -jnp.exp(w_log), k * (1.0 + (iclr - 1.0) * kaw_ref[...]), v,
             kk / jnp.maximum(jnp.sqrt(ss), 1e-12), iclr, _dot(_sigmoid(gd).astype(BF16), wg2_ref[...])], axis=1)
        if single_token:
            ops_ref[:, rows] = packed.T
        else:
            ops_ref[rows, :] = packed

    proj = [project(0)]
    for q in range(n_sub):
        if q + 1 < n_sub:
            proj.append(project(q + 1))
        prev_row = None
        if not single_token:
            prev_row = first if q == 0 else proj[q - 1][2][sub - 1:sub, :]
        finish(q, *proj[q], prev_row)
    pb_last = proj[n_sub - 1][2]
    if single_token:
        last_ref[...] = pb_last
    else:
        carry_ref[0:1, :] = pb_last[sub - 1:sub, :]
        last_ref[0] = pb_last[sub - 8:sub, :]


def _proj_call(x, prev, layer, lw, tm, sub, tiles_per_seq, single_token, norm_input):
    m = x.shape[0]
    nt = m // tm
    row = lambda n: pl.BlockSpec((tm, n), lambda i: (i, 0))
    in_specs = [row(D_MODEL), _const_spec((1, D_MODEL)), _const_spec((1, D_MODEL))]
    args = [x, lw["ln_in_g"], lw["ln_in_b"]]
    if single_token:
        in_specs.append(pl.BlockSpec((None, tm, C_RWKV), lambda i: (layer, i, 0)))
        args.append(prev)
    weights = [lw[n] for n in ("w_in", "b_gate", "ln_v_g", "ln_v_b", "wmix", "bmix", "w_a_out", "mu_shift", "w0",
                               "w_w2p", "a0", "w_a2p", "w_g2", "k_k", "k_a")]
    in_specs += [_layer_spec(w, layer) for w in weights]
    args += weights
    out_shape = [jax.ShapeDtypeStruct((m, D_MODEL), BF16), jax.ShapeDtypeStruct((m, D_MODEL), BF16),
                 jax.ShapeDtypeStruct((m, D_A), F32)]
    out_specs = [row(D_MODEL), row(D_MODEL), row(D_A)]
    if single_token:
        out_shape.append(jax.ShapeDtypeStruct((m, C_RWKV), F32))
        out_specs.append(row(C_RWKV))
    else:
        out_shape.append(jax.ShapeDtypeStruct((nt, 8, C_RWKV), F32))
        out_specs.append(pl.BlockSpec((1, 8, C_RWKV), lambda i: (i, 0, 0)))
    if single_token:
        out_shape.append(jax.ShapeDtypeStruct((N_OPS * D_B, m), F32))
        out_specs.append(pl.BlockSpec((N_OPS * D_B, tm), lambda i: (0, i)))
    else:
        out_shape.append(jax.ShapeDtypeStruct((m, N_OPS * D_B), F32))
        out_specs.append(row(N_OPS * D_B))
    scratch = [] if single_token else [pltpu.VMEM((8, C_RWKV), F32)]
    return pl.pallas_call(
        functools.partial(_proj_kernel, tm=tm, sub=sub, tiles_per_seq=tiles_per_seq, single_token=single_token,
                          norm_input=norm_input),
        grid=(nt,),
        in_specs=in_specs,
        out_specs=out_specs,
        out_shape=out_shape,
        scratch_shapes=scratch,
        compiler_params=pltpu.CompilerParams(dimension_semantics=("arbitrary",), vmem_limit_bytes=VMEM_LIMIT),
        name="proj_single" if single_token else "proj_seq",
    )(*args)


def _rwkv_tile_kernel(ops_ref, lnxg_ref, lnxb_ref, rkw_ref, y_ref, sfin_ref, state_ref, *, n_tile):
    ti = pl.program_id(1)
    n_t = pl.num_programs(1)
    C = RWKV_CHUNK
    RT = RWKV_TILE
    n_ch = RT // C
    n_pair = H_B // 2

    @pl.when(ti == 0)
    def _():
        state_ref[...] = jnp.zeros_like(state_ref)

    row = lax.broadcasted_iota(jnp.int32, (RT, RT), 0)
    col = lax.broadcasted_iota(jnp.int32, (RT, RT), 1)
    same = (row // C) == (col // C)
    incl = jnp.logical_and(same, row >= col)
    strict = jnp.logical_and(same, row > col)
    row_in_chunk = lax.broadcasted_iota(jnp.int32, (RT, 1), 0) % C
    eye = jnp.where(row == col, 1.0, 0.0)
    prow = lax.broadcasted_iota(jnp.int32, (LANES, LANES), 0)
    pcol = lax.broadcasted_iota(jnp.int32, (LANES, LANES), 1)
    pair_bd = (prow // HEAD_B) == (pcol // HEAD_B)
    in_first = lax.broadcasted_iota(jnp.int32, (1, LANES), 1) < HEAD_B
    first_head = jnp.where(in_first, 1.0, 0.0)
    second_head = 1.0 - first_head
    sls = [slice(p * LANES, (p + 1) * LANES) for p in range(n_pair)]
    pr = [h // 2 for h in range(H_B)]

    def op(j, rows, sl):
        return ops_ref[rows, j * D_B + sl.start:j * D_B + sl.stop]

    def prologue(t):
        rows = slice(t * RT, (t + 1) * RT)
        lw = op(OP_LW, rows, slice(0, D_B))
        cum = lw
        for shift in (1, 2, 4, 8, 16, 32):
            cum = cum + jnp.where(row_in_chunk >= shift, pltpu.roll(cum, shift, 0), 0.0)
        ends = [cum[(c + 1) * C - 1:(c + 1) * C, :] for c in range(n_ch)]
        cum_end = jnp.concatenate([jnp.broadcast_to(e, (C, D_B)) for e in ends], axis=0)
        d = dict(rows=rows, ends=ends, atm=[], rtm=[], rt=[], bt=[], kt=[], bh=[], kh=[], v=[])
        for p in range(n_pair):
            sl = sls[p]
            cum_p = cum[:, sl]
            kn_p = op(OP_KN, rows, sl)
            b_p = kn_p * op(OP_IC, rows, sl)
            k_p = op(OP_K, rows, sl)
            e_inv = jnp.exp(-cum_p)
            e_hat = jnp.exp(cum_end[:, sl] - cum_p)
            at_p = -kn_p * jnp.exp(cum_p - lw[:, sl])
            rt_p = op(OP_R, rows, sl) * jnp.exp(cum_p)
            for hm in (first_head, second_head):
                d["atm"].append((at_p * hm).astype(BF16))
                d["rtm"].append((rt_p * hm).astype(BF16))
            d["rt"].append(rt_p)
            d["bt"].append((b_p * e_inv).astype(BF16))
            d["kt"].append((k_p * e_inv).astype(BF16))
            d["bh"].append((b_p * e_hat).astype(BF16))
            d["kh"].append((k_p * e_hat).astype(BF16))
            d["v"].append(op(OP_V, rows, sl).astype(BF16))
        return d

    def main(d):
        atm, rtm, bt, kt, v = d["atm"], d["rtm"], d["bt"], d["kt"], d["v"]
        xs = [jnp.where(strict, _dg(atm[h], bt[pr[h]], 1, 1), 0.0) for h in range(H_B)]
        yield
        ts = [eye + x for x in xs]
        xb = [x.astype(BF16) for x in xs]
        for _ in range(5):
            xb = [_dot(z, z).astype(BF16) for z in xb]
            yield
            ts = [t + _dot(t.astype(BF16), z) for t, z in zip(ts, xb)]
            yield
        tb = [t.astype(BF16) for t in ts]
        a_ak = [jnp.where(strict, _dg(atm[h], kt[pr[h]], 1, 1), 0.0).astype(BF16) for h in range(H_B)]
        yield
        av = [_dot(a_ak[h], v[pr[h]]).astype(BF16) for h in range(H_B)]
        yield
        tx = [_dot(tb[h], jnp.concatenate([av[h], atm[h]], axis=1)) for h in range(H_B)]
        yield
        a_rb = [jnp.where(incl, _dg(rtm[h], bt[pr[h]], 1, 1), 0.0).astype(BF16) for h in range(H_B)]
        yield
        a_rk = [jnp.where(incl, _dg(rtm[h], kt[pr[h]], 1, 1), 0.0).astype(BF16) for h in range(H_B)]
        yield
        rb = [_dot(a_rb[h], tx[h].astype(BF16)) for h in range(H_B)]
        yield
        rkv = [_dot(a_rk[h], v[pr[h]]) for h in range(H_B)]
        d["w2"] = [jnp.where(in_first, tx[2 * p][:, :LANES], tx[2 * p + 1][:, :LANES]).astype(BF16)
                   for p in range(n_pair)]
        d["a_t"] = [(tx[2 * p][:, LANES:] + tx[2 * p + 1][:, LANES:]).astype(BF16) for p in range(n_pair)]
        d["rhat"] = [(d["rt"][p] + (rb[2 * p][:, LANES:] + rb[2 * p + 1][:, LANES:])).astype(BF16)
                     for p in range(n_pair)]
        d["y0"] = [jnp.where(in_first, rb[2 * p][:, :LANES] + rkv[2 * p], rb[2 * p + 1][:, :LANES] + rkv[2 * p + 1])
                   for p in range(n_pair)]
        yield

    def tail(d, st):
        w2, a_t, rhat, y0, bh, kh, v = d["w2"], d["a_t"], d["rhat"], d["y0"], d["bh"], d["kh"], d["v"]
        ys = [[] for _ in range(n_pair)]
        for c in range(n_ch):
            rows = slice(c * C, (c + 1) * C)
            el = [jnp.where(pair_bd, _dg(a_t[p][rows], bh[p][rows], 0, 0), 0.0).astype(BF16) for p in range(n_pair)]
            gm = [jnp.where(pair_bd,
                            _dg(jnp.concatenate([w2[p][rows], v[p][rows]], axis=0),
                                jnp.concatenate([bh[p][rows], kh[p][rows]], axis=0), 0, 0), 0.0)
                  for p in range(n_pair)]
            s = st["s"]
            sb = [z.astype(BF16) for z in s]
            for p in range(n_pair):
                ys[p].append(_dg(rhat[p][rows], sb[p], 1, 1) + y0[p][rows])
            st["s"] = [s[p] * jnp.exp(d["ends"][c][:, sls[p]]) + _dot(sb[p], el[p]) + gm[p] for p in range(n_pair)]
            yield
        rows = d["rows"]
        for p in range(n_pair):
            sl = sls[p]
            y = jnp.concatenate(ys[p], axis=0)
            mu = _pair_head_sum(y) * (1.0 / HEAD_B)
            yc = y - mu
            var = _pair_head_sum(yc * yc) * (1.0 / HEAD_B)
            yn = yc * lax.rsqrt(var + GN_EPS) * lnxg_ref[:, sl] + lnxb_ref[:, sl]
            bonus = (_pair_head_sum(op(OP_R, rows, sl) * op(OP_K, rows, sl) * rkw_ref[:, sl])
                     * op(OP_V, rows, sl))
            y_ref[rows, sl] = (yn + bonus) * op(OP_G, rows, sl)
            yield

    def drain(gen):
        for _ in gen:
            pass

    st = {"s": [state_ref[p] for p in range(n_pair)]}
    pro = prologue(0)
    stages = main(pro)
    next(stages)
    prev_tail = None
    for t in range(n_tile):
        nxt = prologue(t + 1) if t + 1 < n_tile else None
        for n, _ in enumerate(stages):
            if prev_tail is not None and n % 2 == 1:
                next(prev_tail, None)
        if prev_tail is not None:
            drain(prev_tail)
        prev_tail = tail(pro, st)
        if nxt is not None:
            pro = nxt
            stages = main(pro)
            next(stages)
    drain(prev_tail)
    for p in range(n_pair):
        state_ref[p] = st["s"][p]

    @pl.when(ti == n_t - 1)
    def _():
        sfin_ref[0] = state_ref[...]


def _rwkv_seq_call(ops, layer, lw, batch, seq):
    RT = RWKV_TILE * RWKV_TILES_PER_STEP
    n_t = seq // RT
    n_pair = H_B // 2
    blk = pl.BlockSpec((RT, D_B), lambda b, t: (b * n_t + t, 0))
    vec = pl.BlockSpec((None, 1, D_B), lambda b, t: (layer, 0, 0))
    y, s_bd = pl.pallas_call(
        functools.partial(_rwkv_tile_kernel, n_tile=RWKV_TILES_PER_STEP),
        grid=(batch, n_t),
        in_specs=[pl.BlockSpec((RT, N_OPS * D_B), lambda b, t: (b * n_t + t, 0))] + [vec] * 3,
        out_specs=[blk, pl.BlockSpec((1, n_pair, LANES, LANES), lambda b, t: (b, 0, 0, 0))],
        out_shape=[jax.ShapeDtypeStruct((batch * seq, D_B), F32),
                   jax.ShapeDtypeStruct((batch, n_pair, LANES, LANES), F32)],
        scratch_shapes=[pltpu.VMEM((n_pair, LANES, LANES), F32)],
        compiler_params=pltpu.CompilerParams(dimension_semantics=("arbitrary", "arbitrary"),
                                             vmem_limit_bytes=VMEM_LIMIT),
        name="rwkv_seq",
    )(ops, lw["lnx_g"], lw["lnx_b"], lw["r_k"])
    s_bd = s_bd.reshape(batch, n_pair, 2, HEAD_B, 2, HEAD_B)
    s_fin = jnp.stack([s_bd[:, :, 0, :, 0, :], s_bd[:, :, 1, :, 1, :]], axis=2)
    return y, s_fin.reshape(batch, H_B, HEAD_B, HEAD_B)


def _rwkv_step_kernel(s_ref, r_ref, lw_ref, k_ref, v_ref, kn_ref, ic_ref, g_ref, lnxg_ref, lnxb_ref, rkw_ref,
                      buf_ref, y_ref, so_ref, y_scr):
    del buf_ref
    kn = kn_ref[...]
    a = -kn
    b = kn * ic_ref[...]
    w = jnp.exp(lw_ref[...])
    k = k_ref[...]
    r = r_ref[...]
    for i in range(HEAD_B):
        s = s_ref[i]
        sa = jnp.sum(s * a, axis=0, keepdims=True)
        s1 = s * w + sa * b + v_ref[i:i + 1, :] * k
        so_ref[i] = s1
        y_scr[i:i + 1, :] = jnp.sum(s1 * r, axis=0, keepdims=True)
    y = y_scr[...]
    mu = jnp.mean(y, axis=0, keepdims=True)
    yc = y - mu
    var = jnp.mean(yc * yc, axis=0, keepdims=True)
    yn = yc * lax.rsqrt(var + GN_EPS) * lnxg_ref[...] + lnxb_ref[...]
    bonus = jnp.sum(r * k * rkw_ref[...], axis=0, keepdims=True) * v_ref[...]
    y_ref[...] = (yn + bonus) * g_ref[...]


def _rwkv_step_call(state_t, buf, layer, ops_t, lw, nb):
    sblk = pl.BlockSpec((None, None, HEAD_B, HEAD_B, nb), lambda h: (layer, h, 0, 0, 0))
    vblk = pl.BlockSpec((HEAD_B, nb), lambda h: (h, 0))
    pblk = pl.BlockSpec((None, HEAD_B, nb), lambda h: (layer, h, 0))
    return pl.pallas_call(
        _rwkv_step_kernel,
        grid=(H_B,),
        in_specs=[sblk] + [pl.BlockSpec((HEAD_B, nb), lambda h, j=j: (j * H_B + h, 0)) for j in range(N_OPS)]
                 + [pblk] * 3 + [pl.BlockSpec(memory_space=pl.ANY)],
        out_specs=[vblk, sblk],
        out_shape=[jax.ShapeDtypeStruct((D_B, nb), F32), jax.ShapeDtypeStruct(buf.shape, F32)],
        scratch_shapes=[pltpu.VMEM((HEAD_B, nb), F32)],
        input_output_aliases={11: 1},
        compiler_params=pltpu.CompilerParams(dimension_semantics=("arbitrary",)),
        name="rwkv_step",
    )(state_t, *([ops_t] * N_OPS), lw["lnx_g_t"], lw["lnx_b_t"], lw["r_k_t"], buf)


def _route(lg, tm):
    lane = lax.broadcasted_iota(jnp.int32, (tm, ROUTER_LANES), 1)
    lanef = lane.astype(F32)
    big = jnp.float32(1e9)
    ninf = jnp.float32(-jnp.inf)
    isg = lane < N_GROUPS
    lgm = jnp.where(isg, lg, ninf)
    gmax = jnp.max(lgm, axis=-1, keepdims=True)
    den = jnp.sum(jnp.exp(lgm - gmax), axis=-1, keepdims=True)
    gp = 1.0 / den
    gi = jnp.min(jnp.where(lgm == gmax, lanef, big), axis=-1, keepdims=True)
    grp = ((lane - N_GROUPS) >> 2).astype(F32)
    ise = jnp.where(lane >= N_GROUPS, grp, big) == gi
    le = jnp.where(ise, lg, ninf)
    m1 = jnp.max(le, axis=-1, keepdims=True)
    i1 = jnp.min(jnp.where(le == m1, lanef, big), axis=-1, keepdims=True)
    le2 = jnp.where(lanef == i1, ninf, le)
    m2 = jnp.max(le2, axis=-1, keepdims=True)
    i2 = jnp.min(jnp.where(le2 == m2, lanef, big), axis=-1, keepdims=True)
    e2 = jnp.exp(m2 - m1)
    w1 = 1.0 / (1.0 + e2)
    w2 = e2 / (1.0 + e2)
    return jnp.where(lanef == i1, w1 * gp, 0.0) + jnp.where(lanef == i2, w2 * gp, 0.0), gi


def _group_experts(xb, cw, grp, weg_ref, weu_ref, wed_ref):
    hs = []
    for j in range(EXPERTS_PER_GROUP):
        e = grp * EXPERTS_PER_GROUP + j
        hg = _dot(xb, weg_ref[e])
        hu = _dot(xb, weu_ref[e])
        ce = cw[:, N_GROUPS + e:N_GROUPS + e + 1]
        hs.append(((hg * _sigmoid(hg)) * hu * ce).astype(BF16))
    rows = slice(grp * EXPERTS_PER_GROUP * D_EXPERT, (grp + 1) * EXPERTS_PER_GROUP * D_EXPERT)
    return _dot(jnp.concatenate(hs, axis=1), wed_ref[rows, :])


def _experts_dense(x1b, comb, weg_ref, weu_ref, wed_ref):
    moe = _group_experts(x1b, comb, 0, weg_ref, weu_ref, wed_ref)
    for grp in range(1, N_GROUPS):
        moe = moe + _group_experts(x1b, comb, grp, weg_ref, weu_ref, wed_ref)
    return moe


def _experts_sorted(x1b, comb, gi, onehot, tm, weg_ref, weu_ref, wed_ref):
    r = lax.broadcasted_iota(jnp.int32, (tm, tm), 0)
    c = lax.broadcasted_iota(jnp.int32, (tm, tm), 1)
    earlier = jnp.where(r > c, 1.0, 0.0).astype(BF16)
    rank = jnp.sum(onehot * _dot(earlier, onehot.astype(BF16)), axis=-1, keepdims=True)
    key = rank + float(tm) * gi
    key_row = jnp.transpose(jnp.broadcast_to(key, (tm, LANES)))[0:1, :]
    slot = lax.broadcasted_iota(jnp.int32, (MOE_CAP, 1), 0).astype(F32)
    ch, cl = _split(comb)
    moe = jnp.zeros((tm, D_MODEL), F32)
    for grp in range(N_GROUPS):
        sel = jnp.where(key_row == slot + float(tm * grp), 1.0, 0.0).astype(BF16)
        xg = _dot(sel, x1b).astype(BF16)
        cg = _dot(sel, ch) + _dot(sel, cl)
        out = _group_experts(xg, cg, grp, weg_ref, weu_ref, wed_ref)
        moe = moe + _dg(sel, out.astype(BF16), 0, 0)
    return moe


def _mix_kernel(x_ref, lning_ref, lninb_ref, ga_ref, gb_ref, yb_ref, pe_ref, wbout_ref, wo_ref, ln1g_ref,
                ln1b_ref, wr_ref, br_ref, weg_ref, weu_ref, wed_ref, wpg_ref, wpp_ref, ln2g_ref, ln2b_ref,
                o_ref, *scratch, tm, dn_alpha, norm_input, yb_on_lanes):
    x = x_ref[...]
    if norm_input:
        x = _ln(x, lning_ref[...], lninb_ref[...], LN_EPS)
    if yb_on_lanes:
        bo = _dg(yb_ref[...].astype(BF16), wbout_ref[...], 0, 0)
    else:
        bo = _dot(yb_ref[...].astype(BF16), wbout_ref[...])
    merged = ga_ref[...] + gb_ref[...] * bo
    mix = _dot(merged.astype(BF16), wo_ref[...])
    x1 = _ln(dn_alpha * x + mix, ln1g_ref[...], ln1b_ref[...], LN_EPS)
    x1b = x1.astype(BF16)

    x1h, x1l = _split(x1)
    wrh = wr_ref[0]
    wrl = wr_ref[1]
    lg = _dot(x1h, wrh) + (_dot(x1h, wrl) + _dot(x1l, wrh)) + br_ref[...]
    comb, gi = _route(lg, tm)

    if scratch:
        (moe_ref,) = scratch
        lane = lax.broadcasted_iota(jnp.int32, (tm, ROUTER_LANES), 1).astype(F32)
        onehot = jnp.where(lane == gi, 1.0, 0.0)
        fits = jnp.max(jnp.sum(onehot, axis=0, keepdims=True)) <= MOE_CAP

        @pl.when(fits)
        def _():
            moe_ref[...] = _experts_sorted(x1b, comb, gi, onehot, tm, weg_ref, weu_ref, wed_ref)

        @pl.when(jnp.logical_not(fits))
        def _():
            moe_ref[...] = _experts_dense(x1b, comb, weg_ref, weu_ref, wed_ref)

        moe = moe_ref[...]
    else:
        moe = _experts_dense(x1b, comb, weg_ref, weu_ref, wed_ref)

    ple = _sigmoid(_dot(x1b, wpg_ref[...])) * _dot(pe_ref[...].astype(BF16), wpp_ref[...])
    o_ref[...] = _ln(dn_alpha * x1 + moe + ple, ln2g_ref[...], ln2b_ref[...], LN_EPS)


def _mix_call(x, ga, gb, yb, pe, layer, lw, tm, dn_alpha, norm_input, yb_on_lanes):
    m = x.shape[0]
    row = lambda n: pl.BlockSpec((tm, n), lambda i: (i, 0))
    yb_spec = pl.BlockSpec((D_B, tm), lambda i: (0, i)) if yb_on_lanes else row(D_B)
    weights = [lw[n] for n in ("w_b_out", "w_o", "ln1_g", "ln1_b", "w_router", "b_router", "w_e_gate", "w_e_up",
                               "w_e_down", "w_pe_gate", "w_pe_proj", "ln2_g", "ln2_b")]
    return pl.pallas_call(
        functools.partial(_mix_kernel, tm=tm, dn_alpha=dn_alpha, norm_input=norm_input, yb_on_lanes=yb_on_lanes),
        grid=(m // tm,),
        in_specs=[row(D_MODEL), _const_spec((1, D_MODEL)), _const_spec((1, D_MODEL)), row(D_MODEL), row(D_MODEL),
                  yb_spec, pl.BlockSpec((None, tm, PLE_DIM), lambda i: (layer, i, 0))]
                 + [_layer_spec(w, layer) for w in weights],
        out_specs=row(D_MODEL),
        out_shape=jax.ShapeDtypeStruct((m, D_MODEL), F32),
        scratch_shapes=[pltpu.VMEM((tm, D_MODEL), F32)] if tm > MOE_CAP else [],
        compiler_params=pltpu.CompilerParams(dimension_semantics=("parallel",), vmem_limit_bytes=VMEM_LIMIT),
        name="mix_moe",
    )(x, lw["ln_in_g"], lw["ln_in_b"], ga, gb, yb, pe, *weights)


def _prep_weights(p, nb):
    depth = p["w_in"].shape[0]
    vec = lambda z: z.reshape(depth, 1, -1)
    bf = lambda z: z.astype(BF16)
    lanes = lambda z: jnp.broadcast_to(z.reshape(depth, D_B, 1), (depth, D_B, nb))
    zeros64 = jnp.zeros((depth, LORA_W, D_B), F32)
    pad = ROUTER_LANES - N_GROUPS - N_EXPERTS
    w_router = jnp.concatenate(
        [p["w_rg"], jnp.transpose(p["w_re"], (0, 2, 1, 3)).reshape(depth, D_MODEL, N_EXPERTS),
         jnp.zeros((depth, D_MODEL, pad), F32)], axis=2)
    wr_hi = w_router.astype(BF16)
    wr_lo = (w_router - wr_hi.astype(F32)).astype(BF16)
    b_router = jnp.concatenate([p["b_rg"], p["b_re"].reshape(depth, N_EXPERTS), jnp.zeros((depth, pad), F32)], axis=1)
    eye = jnp.eye(CHUNK, dtype=F32)
    w = {n: vec(p[n]) for n in ("b_gate", "ln_v_g", "ln_v_b", "mu_shift", "w0", "a0", "k_k", "k_a", "lnx_g", "lnx_b",
                                "ln1_g", "ln1_b", "ln2_g", "ln2_b")}
    w.update({n: bf(p[n]) for n in ("w_in", "w_a_out", "w_g2", "w_b_out", "w_o", "w_e_gate", "w_e_up", "w_pe_gate",
                                    "w_pe_proj")})
    w.update({
        "w_w2p": bf(jnp.concatenate([p["w_w2"], zeros64], axis=1)),
        "w_a2p": bf(jnp.concatenate([zeros64, p["w_a2"]], axis=1)),
        "r_k": vec(p["r_k"]),
        "lnx_g_t": lanes(p["lnx_g"]), "lnx_b_t": lanes(p["lnx_b"]), "r_k_t": lanes(p["r_k"]),
        "w_router": jnp.stack([wr_hi, wr_lo], axis=1), "b_router": vec(b_router),
        "w_e_down": bf(p["w_e_down"]).reshape(depth, N_EXPERTS * D_EXPERT, D_MODEL),
        "wmix_seq": p["w_s"],
        "bmix_seq": jnp.repeat(jnp.transpose(p["b_s"], (0, 2, 1)), LANES, axis=2),
        "wmix_one": p["w_s"][:, :, 0, 0][:, :, None, None] * eye,
        "bmix_one": jnp.broadcast_to(jnp.repeat(p["b_s"][:, :, 0], LANES, axis=1)[:, None, :], (depth, CHUNK, D_A)),
    })
    return w


def kernel(x_prompt, x_sample, state_rwkv, state_shift, p_prompt, p_sample, ln_in_g, ln_in_b, w_in, b_gate,
           ln_v_g, ln_v_b, w_s, b_s, w_a_out, mu_shift, w0, w_w2, a0, w_a2, w_g2, k_k, k_a, r_k, lnx_g, lnx_b,
           w_b_out, w_o, ln1_g, ln1_b, w_rg, b_rg, w_re, b_re, w_e_gate, w_e_up, w_e_down, w_pe_gate,
           w_pe_proj, ln2_g, ln2_b):
    p = dict(w_in=w_in, b_gate=b_gate, ln_v_g=ln_v_g, ln_v_b=ln_v_b, w_s=w_s, b_s=b_s, w_a_out=w_a_out,
             mu_shift=mu_shift, w0=w0, w_w2=w_w2, a0=a0, w_a2=w_a2, w_g2=w_g2, k_k=k_k, k_a=k_a, r_k=r_k,
             lnx_g=lnx_g, lnx_b=lnx_b, w_b_out=w_b_out, w_o=w_o, ln1_g=ln1_g, ln1_b=ln1_b, w_rg=w_rg, b_rg=b_rg,
             w_re=w_re, b_re=b_re, w_e_gate=w_e_gate, w_e_up=w_e_up, w_e_down=w_e_down,
             w_pe_gate=w_pe_gate, w_pe_proj=w_pe_proj, ln2_g=ln2_g, ln2_b=ln2_b)
    batch, seq, _ = x_prompt.shape
    nb, dec_seq, _ = x_sample.shape
    depth = w_in.shape[0]
    assert dec_seq == 1 and seq % CHUNK == 0 and seq % (RWKV_TILE * RWKV_TILES_PER_STEP) == 0 and nb % LANES == 0
    dn_alpha = (2 * depth) ** 0.25
    tm = 512
    tms = nb
    tiles_per_seq = seq // tm
    last_chunk = ((seq - 1) // CHUNK) * CHUNK

    xp = x_prompt.reshape(batch * seq, D_MODEL)
    xs = x_sample.reshape(nb, D_MODEL)
    pe_p = p_prompt.reshape(depth, batch * seq, PLE_DIM)
    pe_s = p_sample.reshape(depth, nb, PLE_DIM)
    lw = _prep_weights(p, nb)
    lw["ln_in_g"] = ln_in_g.reshape(1, D_MODEL)
    lw["ln_in_b"] = ln_in_b.reshape(1, D_MODEL)
    lw_p = dict(lw, wmix=lw["wmix_seq"], bmix=lw["bmix_seq"])
    lw_s = dict(lw, wmix=lw["wmix_one"], bmix=lw["bmix_one"])
    state_t = jnp.transpose(state_rwkv, (0, 2, 3, 4, 1))
    state_out = jnp.zeros_like(state_t)

    outs = [[] for _ in range(5)]
    for i in range(depth):
        first = i == 0
        res = _proj_call(xp, None, i, lw_p, tm, PROJ_SUB, tiles_per_seq, False, first)
        ga, gb, va, last, ops = res
        yb, s_fin = _rwkv_seq_call(ops, i, lw, batch, seq)
        xp = _mix_call(xp, ga, gb, yb, pe_p, i, lw, tm, dn_alpha, first, False)
        outs[0].append(s_fin)
        outs[1].append(last[tiles_per_seq - 1::tiles_per_seq, 7, :])
        outs[2].append(va.reshape(batch, seq, D_A)[:, last_chunk:])

        res = _proj_call(xs, state_shift, i, lw_s, tms, tms, 1, True, first)
        ga, gb, va, pb, ops_t = res
        yb_t, state_out = _rwkv_step_call(state_t, state_out, i, ops_t, lw, nb)
        xs = _mix_call(xs, ga, gb, yb_t, pe_s, i, lw, tms, dn_alpha, first, True)
        outs[3].append(pb)
        outs[4].append(va.reshape(nb, 1, D_A))

    return (xp.reshape(batch, seq, D_MODEL), xs.reshape(nb, 1, D_MODEL),
            jnp.stack(outs[0]), jnp.stack(outs[1]), jnp.stack(outs[2]),
            jnp.transpose(state_out, (0, 4, 1, 2, 3)), jnp.stack(outs[3]), jnp.stack(outs[4]))
```

```python
import functools

import jax
import jax.numpy as jnp
from jax import lax
from jax.experimental import pallas as pl
from jax.experimental.pallas import tpu as pltpu

F32 = jnp.float32
BF16 = jnp.bfloat16

D_MODEL = 1024
CHUNK = 128
D_A = 512
G_A = 4
D_B = 512
HEAD_B = 64
H_B = D_B // HEAD_B
LORA_W = 64
LORA_A = 64
LORA_G = 128
C_A = 2 * D_A
C_RWKV = 3 * D_B + LORA_W + LORA_A + LORA_G
C_GATE = 2 * D_MODEL
C_IN = C_A + C_RWKV + C_GATE
N_GROUPS = 4
EXPERTS_PER_GROUP = 4
N_EXPERTS = N_GROUPS * EXPERTS_PER_GROUP
D_EXPERT = 256
PLE_DIM = 256
LN_EPS = 1e-5
GN_EPS = 64e-5

LANES = 128
ROUTER_LANES = LANES
RWKV_CHUNK = 64
RWKV_TILE = 256
RWKV_TILES_PER_STEP = 2
OP_R, OP_LW, OP_K, OP_V, OP_KN, OP_IC, OP_G = range(7)
N_OPS = 7
MOE_CAP = 192
PROJ_SUB = 128
VMEM_LIMIT = 56 * 1024 * 1024


def _ln(x, g, b, eps):
    mu = jnp.mean(x, axis=-1, keepdims=True)
    xc = x - mu
    var = jnp.mean(xc * xc, axis=-1, keepdims=True)
    return xc * lax.rsqrt(var + eps) * g + b


def _dot(a, b):
    return jnp.dot(a, b, preferred_element_type=F32)


def _split(x):
    hi = x.astype(BF16)
    lo = (x - hi.astype(F32)).astype(BF16)
    return hi, lo


def _dg(a, b, ca, cb):
    return lax.dot_general(a, b, (((ca,), (cb,)), ((), ())), preferred_element_type=F32)


def _gelu_tanh(x):
    return 0.5 * x * (1.0 + jnp.tanh(0.7978845608028654 * (x + 0.044715 * (x * x * x))))


def _sigmoid(x):
    return 1.0 / (1.0 + jnp.exp(-x))


def _softplus(x):
    return jnp.maximum(x, 0.0) + jnp.log(1.0 + jnp.exp(-jnp.abs(x)))


def _pair_head_sum(z):
    in_first = lax.broadcasted_iota(jnp.int32, (1, LANES), 1) < HEAD_B
    s0 = jnp.sum(jnp.where(in_first, z, 0.0), axis=-1, keepdims=True)
    s1 = jnp.sum(jnp.where(in_first, 0.0, z), axis=-1, keepdims=True)
    return jnp.where(in_first, s0, s1)


def _const_spec(shape):
    nd = len(shape)
    return pl.BlockSpec(shape, lambda *_: (0,) * nd, pipeline_mode=pl.Buffered(1))


def _layer_spec(arr, layer):
    nd = arr.ndim - 1
    return pl.BlockSpec((None,) + arr.shape[1:], lambda *_: (layer,) + (0,) * nd, pipeline_mode=pl.Buffered(1))


def _proj_kernel(*refs, tm, sub, tiles_per_seq, single_token, norm_input):
    if single_token:
        (x_ref, lning_ref, lninb_ref, prev_ref, win_ref, bgate_ref, lnvg_ref, lnvb_ref, wmix_ref, bmix_ref,
         waout_ref, mu_ref, w0_ref, ww2_ref, a0_ref, wa2_ref, wg2_ref, kkw_ref, kaw_ref,
         ga_ref, gb_ref, va_ref, last_ref, ops_ref) = refs
    else:
        (x_ref, lning_ref, lninb_ref, win_ref, bgate_ref, lnvg_ref, lnvb_ref, wmix_ref, bmix_ref,
         waout_ref, mu_ref, w0_ref, ww2_ref, a0_ref, wa2_ref, wg2_ref, kkw_ref, kaw_ref,
         ga_ref, gb_ref, va_ref, last_ref, ops_ref, carry_ref) = refs

    n_sub = tm // sub
    rr = lax.broadcasted_iota(jnp.int32, (CHUNK, CHUNK), 0)
    cc = lax.broadcasted_iota(jnp.int32, (CHUNK, CHUNK), 1)
    wmix = [jnp.where(rr >= cc, wmix_ref[g], 0.0).astype(BF16) for g in range(G_A)]
    if not single_token:
        i = pl.program_id(0)

        @pl.when(i == 0)
        def _():
            carry_ref[...] = jnp.zeros_like(carry_ref)

        first = jnp.where(i % tiles_per_seq == 0, jnp.zeros((1, C_RWKV), F32), carry_ref[0:1, :])
        row = lax.broadcasted_iota(jnp.int32, (sub, C_RWKV), 0)

    def project(q):
        x = x_ref[q * sub:(q + 1) * sub, :]
        if norm_input:
            x = _ln(x, lning_ref[...], lninb_ref[...], LN_EPS)
        xb = x.astype(BF16)
        return (_dot(xb, win_ref[:, 0:C_A]), _dot(xb, win_ref[:, C_A + C_RWKV:]),
                _dot(xb, win_ref[:, C_A:C_A + C_RWKV]))

    def finish(q, pa, pg, pb, prev_row):
        rows = slice(q * sub, (q + 1) * sub)
        z = _gelu_tanh(pa)
        u = z[:, :D_A]
        va = _ln(z[:, D_A:], lnvg_ref[...], lnvb_ref[...], LN_EPS)
        va_ref[rows, :] = va
        vab = va.astype(BF16)
        mix = jnp.concatenate(
            [jnp.concatenate([_dot(wmix[g], vab[c * CHUNK:(c + 1) * CHUNK, g * LANES:(g + 1) * LANES])
                              for g in range(G_A)], axis=1) + bmix_ref[...]
             for c in range(sub // CHUNK)], axis=0)
        ao = _dot((u * mix).astype(BF16), waout_ref[...])
        gates = _sigmoid(pg + bgate_ref[...])
        ga_ref[rows, :] = (gates[:, :D_MODEL] * ao).astype(BF16)
        gb_ref[rows, :] = gates[:, D_MODEL:].astype(BF16)

        if single_token:
            prev = prev_ref[rows, :]
        else:
            prev = jnp.where(row == 0, prev_row, pltpu.roll(pb, 1, 0))
        xs = pb + mu_ref[...] * (prev - pb)
        r = xs[:, 0:D_B]
        k = xs[:, D_B:2 * D_B]
        v = xs[:, 2 * D_B:3 * D_B]
        o = 3 * D_B
        wa_d = xs[:, o:o + LORA_W + LORA_A]
        gd = xs[:, o + LORA_W + LORA_A:]
        w_log = -_softplus(-(w0_ref[...] + _dot(jnp.tanh(wa_d).astype(BF16), ww2_ref[...]))) - 0.5
        iclr = _sigmoid(a0_ref[...] + _dot(wa_d.astype(BF16), wa2_ref[...]))
        kk = k * kkw_ref[...]
        sq = kk * kk
        ss = jnp.concatenate([_pair_head_sum(sq[:, p * LANES:(p + 1) * LANES]) for p in range(D_B // LANES)], axis=1)
        packed = jnp.concatenate(
            [r, -jnp.exp(w_log), k * (1.0 + (iclr - 1.0) * kaw_ref[...]), v,
             kk / jnp.maximum(jnp.sqrt(ss), 1e-12), iclr, _dot(_sigmoid(gd).astype(BF16), wg2_ref[...])], axis=1)
        if single_token:
            ops_ref[:, rows] = packed.T
        else:
            ops_ref[rows, :] = packed

    proj = [project(0)]
    for q in range(n_sub):
        if q + 1 < n_sub:
            proj.append(project(q + 1))
        prev_row = None
        if not single_token:
            prev_row = first if q == 0 else proj[q - 1][2][sub - 1:sub, :]
        finish(q, *proj[q], prev_row)
    pb_last = proj[n_sub - 1][2]
    if single_token:
        last_ref[...] = pb_last
    else:
        carry_ref[0:1, :] = pb_last[sub - 1:sub, :]
        last_ref[0] = pb_last[sub - 8:sub, :]


def _proj_call(x, prev, layer, lw, tm, sub, tiles_per_seq, single_token, norm_input):
    m = x.shape[0]
    nt = m // tm
    row = lambda n: pl.BlockSpec((tm, n), lambda i: (i, 0))
    in_specs = [row(D_MODEL), _const_spec((1, D_MODEL)), _const_spec((1, D_MODEL))]
    args = [x, lw["ln_in_g"], lw["ln_in_b"]]
    if single_token:
        in_specs.append(pl.BlockSpec((None, tm, C_RWKV), lambda i: (layer, i, 0)))
        args.append(prev)
    weights = [lw[n] for n in ("w_in", "b_gate", "ln_v_g", "ln_v_b", "wmix", "bmix", "w_a_out", "mu_shift", "w0",
                               "w_w2p", "a0", "w_a2p", "w_g2", "k_k", "k_a")]
    in_specs += [_layer_spec(w, layer) for w in weights]
    args += weights
    out_shape = [jax.ShapeDtypeStruct((m, D_MODEL), BF16), jax.ShapeDtypeStruct((m, D_MODEL), BF16),
                 jax.ShapeDtypeStruct((m, D_A), F32)]
    out_specs = [row(D_MODEL), row(D_MODEL), row(D_A)]
    if single_token:
        out_shape.append(jax.ShapeDtypeStruct((m, C_RWKV), F32))
        out_specs.append(row(C_RWKV))
    else:
        out_shape.append(jax.ShapeDtypeStruct((nt, 8, C_RWKV), F32))
        out_specs.append(pl.BlockSpec((1, 8, C_RWKV), lambda i: (i, 0, 0)))
    if single_token:
        out_shape.append(jax.ShapeDtypeStruct((N_OPS * D_B, m), F32))
        out_specs.append(pl.BlockSpec((N_OPS * D_B, tm), lambda i: (0, i)))
    else:
        out_shape.append(jax.ShapeDtypeStruct((m, N_OPS * D_B), F32))
        out_specs.append(row(N_OPS * D_B))
    scratch = [] if single_token else [pltpu.VMEM((8, C_RWKV), F32)]
    return pl.pallas_call(
        functools.partial(_proj_kernel, tm=tm, sub=sub, tiles_per_seq=tiles_per_seq, single_token=single_token,
                          norm_input=norm_input),
        grid=(nt,),
        in_specs=in_specs,
        out_specs=out_specs,
        out_shape=out_shape,
        scratch_shapes=scratch,
        compiler_params=pltpu.CompilerParams(dimension_semantics=("arbitrary",), vmem_limit_bytes=VMEM_LIMIT),
        name="proj_single" if single_token else "proj_seq",
    )(*args)


def _rwkv_tile_kernel(ops_ref, lnxg_ref, lnxb_ref, rkw_ref, y_ref, sfin_ref, state_ref, *, n_tile):
    ti = pl.program_id(1)
    n_t = pl.num_programs(1)
    C = RWKV_CHUNK
    RT = RWKV_TILE
    n_ch = RT // C
    n_pair = H_B // 2

    @pl.when(ti == 0)
    def _():
        state_ref[...] = jnp.zeros_like(state_ref)

    row = lax.broadcasted_iota(jnp.int32, (RT, RT), 0)
    col = lax.broadcasted_iota(jnp.int32, (RT, RT), 1)
    same = (row // C) == (col // C)
    incl = jnp.logical_and(same, row >= col)
    strict = jnp.logical_and(same, row > col)
    row_in_chunk = lax.broadcasted_iota(jnp.int32, (RT, 1), 0) % C
    eye = jnp.where(row == col, 1.0, 0.0)
    prow = lax.broadcasted_iota(jnp.int32, (LANES, LANES), 0)
    pcol = lax.broadcasted_iota(jnp.int32, (LANES, LANES), 1)
    pair_bd = (prow // HEAD_B) == (pcol // HEAD_B)
    in_first = lax.broadcasted_iota(jnp.int32, (1, LANES), 1) < HEAD_B
    first_head = jnp.where(in_first, 1.0, 0.0)
    second_head = 1.0 - first_head
    sls = [slice(p * LANES, (p + 1) * LANES) for p in range(n_pair)]
    pr = [h // 2 for h in range(H_B)]

    def op(j, rows, sl):
        return ops_ref[rows, j * D_B + sl.start:j * D_B + sl.stop]

    def prologue(t):
        rows = slice(t * RT, (t + 1) * RT)
        lw = op(OP_LW, rows, slice(0, D_B))
        cum = lw
        for shift in (1, 2, 4, 8, 16, 32):
            cum = cum + jnp.where(row_in_chunk >= shift, pltpu.roll(cum, shift, 0), 0.0)
        ends = [cum[(c + 1) * C - 1:(c + 1) * C, :] for c in range(n_ch)]
        cum_end = jnp.concatenate([jnp.broadcast_to(e, (C, D_B)) for e in ends], axis=0)
        d = dict(rows=rows, ends=ends, atm=[], rtm=[], rt=[], bt=[], kt=[], bh=[], kh=[], v=[])
        for p in range(n_pair):
            sl = sls[p]
            cum_p = cum[:, sl]
            kn_p = op(OP_KN, rows, sl)
            b_p = kn_p * op(OP_IC, rows, sl)
            k_p = op(OP_K, rows, sl)
            e_inv = jnp.exp(-cum_p)
            e_hat = jnp.exp(cum_end[:, sl] - cum_p)
            at_p = -kn_p * jnp.exp(cum_p - lw[:, sl])
            rt_p = op(OP_R, rows, sl) * jnp.exp(cum_p)
            for hm in (first_head, second_head):
                d["atm"].append((at_p * hm).astype(BF16))
                d["rtm"].append((rt_p * hm).astype(BF16))
            d["rt"].append(rt_p)
            d["bt"].append((b_p * e_inv).astype(BF16))
            d["kt"].append((k_p * e_inv).astype(BF16))
            d["bh"].append((b_p * e_hat).astype(BF16))
            d["kh"].append((k_p * e_hat).astype(BF16))
            d["v"].append(op(OP_V, rows, sl).astype(BF16))
        return d

    def main(d):
        atm, rtm, bt, kt, v = d["atm"], d["rtm"], d["bt"], d["kt"], d["v"]
        xs = [jnp.where(strict, _dg(atm[h], bt[pr[h]], 1, 1), 0.0) for h in range(H_B)]
        yield
        ts = [eye + x for x in xs]
        xb = [x.astype(BF16) for x in xs]
        for _ in range(5):
            xb = [_dot(z, z).astype(BF16) for z in xb]
            yield
            ts = [t + _dot(t.astype(BF16), z) for t, z in zip(ts, xb)]
            yield
        tb = [t.astype(BF16) for t in ts]
        a_ak = [jnp.where(strict, _dg(atm[h], kt[pr[h]], 1, 1), 0.0).astype(BF16) for h in range(H_B)]
        yield
        av = [_dot(a_ak[h], v[pr[h]]).astype(BF16) for h in range(H_B)]
        yield
        tx = [_dot(tb[h], jnp.concatenate([av[h], atm[h]], axis=1)) for h in range(H_B)]
        yield
        a_rb = [jnp.where(incl, _dg(rtm[h], bt[pr[h]], 1, 1), 0.0).astype(BF16) for h in range(H_B)]
        yield
        a_rk = [jnp.where(incl, _dg(rtm[h], kt[pr[h]], 1, 1), 0.0).astype(BF16) for h in range(H_B)]
        yield
        rb = [_dot(a_rb[h], tx[h].astype(BF16)) for h in range(H_B)]
        yield
        rkv = [_dot(a_rk[h], v[pr[h]]) for h in range(H_B)]
        d["w2"] = [jnp.where(in_first, tx[2 * p][:, :LANES], tx[2 * p + 1][:, :LANES]).astype(BF16)
                   for p in range(n_pair)]
        d["a_t"] = [(tx[2 * p][:, LANES:] + tx[2 * p + 1][:, LANES:]).astype(BF16) for p in range(n_pair)]
        d["rhat"] = [(d["rt"][p] + (rb[2 * p][:, LANES:] + rb[2 * p + 1][:, LANES:])).astype(BF16)
                     for p in range(n_pair)]
        d["y0"] = [jnp.where(in_first, rb[2 * p][:, :LANES] + rkv[2 * p], rb[2 * p + 1][:, :LANES] + rkv[2 * p + 1])
                   for p in range(n_pair)]
        yield

    def tail(d, st):
        w2, a_t, rhat, y0, bh, kh, v = d["w2"], d["a_t"], d["rhat"], d["y0"], d["bh"], d["kh"], d["v"]
        ys = [[] for _ in range(n_pair)]
        for c in range(n_ch):
            rows = slice(c * C, (c + 1) * C)
            el = [jnp.where(pair_bd, _dg(a_t[p][rows], bh[p][rows], 0, 0), 0.0).astype(BF16) for p in range(n_pair)]
            gm = [jnp.where(pair_bd,
                            _dg(jnp.concatenate([w2[p][rows], v[p][rows]], axis=0),
                                jnp.concatenate([bh[p][rows], kh[p][rows]], axis=0), 0, 0), 0.0)
                  for p in range(n_pair)]
            s = st["s"]
            sb = [z.astype(BF16) for z in s]
            for p in range(n_pair):
                ys[p].append(_dg(rhat[p][rows], sb[p], 1, 1) + y0[p][rows])
            st["s"] = [s[p] * jnp.exp(d["ends"][c][:, sls[p]]) + _dot(sb[p], el[p]) + gm[p] for p in range(n_pair)]
            yield
        rows = d["rows"]
        for p in range(n_pair):
            sl = sls[p]
            y = jnp.concatenate(ys[p], axis=0)
            mu = _pair_head_sum(y) * (1.0 / HEAD_B)
            yc = y - mu
            var = _pair_head_sum(yc * yc) * (1.0 / HEAD_B)
            yn = yc * lax.rsqrt(var + GN_EPS) * lnxg_ref[:, sl] + lnxb_ref[:, sl]
            bonus = (_pair_head_sum(op(OP_R, rows, sl) * op(OP_K, rows, sl) * rkw_ref[:, sl])
                     * op(OP_V, rows, sl))
            y_ref[rows, sl] = (yn + bonus) * op(OP_G, rows, sl)
            yield

    def drain(gen):
        for _ in gen:
            pass

    st = {"s": [state_ref[p] for p in range(n_pair)]}
    pro = prologue(0)
    stages = main(pro)
    next(stages)
    prev_tail = None
    for t in range(n_tile):
        nxt = prologue(t + 1) if t + 1 < n_tile else None
        for n, _ in enumerate(stages):
            if prev_tail is not None and n % 2 == 1:
                next(prev_tail, None)
        if prev_tail is not None:
            drain(prev_tail)
        prev_tail = tail(pro, st)
        if nxt is not None:
            pro = nxt
            stages = main(pro)
            next(stages)
    drain(prev_tail)
    for p in range(n_pair):
        state_ref[p] = st["s"][p]

    @pl.when(ti == n_t - 1)
    def _():
        sfin_ref[0] = state_ref[...]


def _rwkv_seq_call(ops, layer, lw, batch, seq):
    RT = RWKV_TILE * RWKV_TILES_PER_STEP
    n_t = seq // RT
    n_pair = H_B // 2
    blk = pl.BlockSpec((RT, D_B), lambda b, t: (b * n_t + t, 0))
    vec = pl.BlockSpec((None, 1, D_B), lambda b, t: (layer, 0, 0))
    y, s_bd = pl.pallas_call(
        functools.partial(_rwkv_tile_kernel, n_tile=RWKV_TILES_PER_STEP),
        grid=(batch, n_t),
        in_specs=[pl.BlockSpec((RT, N_OPS * D_B), lambda b, t: (b * n_t + t, 0))] + [vec] * 3,
        out_specs=[blk, pl.BlockSpec((1, n_pair, LANES, LANES), lambda b, t: (b, 0, 0, 0))],
        out_shape=[jax.ShapeDtypeStruct((batch * seq, D_B), F32),
                   jax.ShapeDtypeStruct((batch, n_pair, LANES, LANES), F32)],
        scratch_shapes=[pltpu.VMEM((n_pair, LANES, LANES), F32)],
        compiler_params=pltpu.CompilerParams(dimension_semantics=("arbitrary", "arbitrary"),
                                             vmem_limit_bytes=VMEM_LIMIT),
        name="rwkv_seq",
    )(ops, lw["lnx_g"], lw["lnx_b"], lw["r_k"])
    s_bd = s_bd.reshape(batch, n_pair, 2, HEAD_B, 2, HEAD_B)
    s_fin = jnp.stack([s_bd[:, :, 0, :, 0, :], s_bd[:, :, 1, :, 1, :]], axis=2)
    return y, s_fin.reshape(batch, H_B, HEAD_B, HEAD_B)


def _rwkv_step_kernel(s_ref, r_ref, lw_ref, k_ref, v_ref, kn_ref, ic_ref, g_ref, lnxg_ref, lnxb_ref, rkw_ref,
                      buf_ref, y_ref, so_ref, y_scr):
    del buf_ref
    kn = kn_ref[...]
    a = -kn
    b = kn * ic_ref[...]
    w = jnp.exp(lw_ref[...])
    k = k_ref[...]
    r = r_ref[...]
    for i in range(HEAD_B):
        s = s_ref[i]
        sa = jnp.sum(s * a, axis=0, keepdims=True)
        s1 = s * w + sa * b + v_ref[i:i + 1, :] * k
        so_ref[i] = s1
        y_scr[i:i + 1, :] = jnp.sum(s1 * r, axis=0, keepdims=True)
    y = y_scr[...]
    mu = jnp.mean(y, axis=0, keepdims=True)
    yc = y - mu
    var = jnp.mean(yc * yc, axis=0, keepdims=True)
    yn = yc * lax.rsqrt(var + GN_EPS) * lnxg_ref[...] + lnxb_ref[...]
    bonus = jnp.sum(r * k * rkw_ref[...], axis=0, keepdims=True) * v_ref[...]
    y_ref[...] = (yn + bonus) * g_ref[...]


def _rwkv_step_call(state_t, buf, layer, ops_t, lw, nb):
    sblk = pl.BlockSpec((None, None, HEAD_B, HEAD_B, nb), lambda h: (layer, h, 0, 0, 0))
    vblk = pl.BlockSpec((HEAD_B, nb), lambda h: (h, 0))
    pblk = pl.BlockSpec((None, HEAD_B, nb), lambda h: (layer, h, 0))
    return pl.pallas_call(
        _rwkv_step_kernel,
        grid=(H_B,),
        in_specs=[sblk] + [pl.BlockSpec((HEAD_B, nb), lambda h, j=j: (j * H_B + h, 0)) for j in range(N_OPS)]
                 + [pblk] * 3 + [pl.BlockSpec(memory_space=pl.ANY)],
        out_specs=[vblk, sblk],
        out_shape=[jax.ShapeDtypeStruct((D_B, nb), F32), jax.ShapeDtypeStruct(buf.shape, F32)],
        scratch_shapes=[pltpu.VMEM((HEAD_B, nb), F32)],
        input_output_aliases={11: 1},
        compiler_params=pltpu.CompilerParams(dimension_semantics=("arbitrary",)),
        name="rwkv_step",
    )(state_t, *([ops_t] * N_OPS), lw["lnx_g_t"], lw["lnx_b_t"], lw["r_k_t"], buf)


def _route(lg, tm):
    lane = lax.broadcasted_iota(jnp.int32, (tm, ROUTER_LANES), 1)
    lanef = lane.astype(F32)
    big = jnp.float32(1e9)
    ninf = jnp.float32(-jnp.inf)
    isg = lane < N_GROUPS
    lgm = jnp.where(isg, lg, ninf)
    gmax = jnp.max(lgm, axis=-1, keepdims=True)
    den = jnp.sum(jnp.exp(lgm - gmax), axis=-1, keepdims=True)
    gp = 1.0 / den
    gi = jnp.min(jnp.where(lgm == gmax, lanef, big), axis=-1, keepdims=True)
    grp = ((lane - N_GROUPS) >> 2).astype(F32)
    ise = jnp.where(lane >= N_GROUPS, grp, big) == gi
    le = jnp.where(ise, lg, ninf)
    m1 = jnp.max(le, axis=-1, keepdims=True)
    i1 = jnp.min(jnp.where(le == m1, lanef, big), axis=-1, keepdims=True)
    le2 = jnp.where(lanef == i1, ninf, le)
    m2 = jnp.max(le2, axis=-1, keepdims=True)
    i2 = jnp.min(jnp.where(le2 == m2, lanef, big), axis=-1, keepdims=True)
    e2 = jnp.exp(m2 - m1)
    w1 = 1.0 / (1.0 + e2)
    w2 = e2 / (1.0 + e2)
    return jnp.where(lanef == i1, w1 * gp, 0.0) + jnp.where(lanef == i2, w2 * gp, 0.0), gi


def _group_experts(xb, cw, grp, weg_ref, weu_ref, wed_ref):
    hs = []
    for j in range(EXPERTS_PER_GROUP):
        e = grp * EXPERTS_PER_GROUP + j
        hg = _dot(xb, weg_ref[e])
        hu = _dot(xb, weu_ref[e])
        ce = cw[:, N_GROUPS + e:N_GROUPS + e + 1]
        hs.append(((hg * _sigmoid(hg)) * hu * ce).astype(BF16))
    rows = slice(grp * EXPERTS_PER_GROUP * D_EXPERT, (grp + 1) * EXPERTS_PER_GROUP * D_EXPERT)
    return _dot(jnp.concatenate(hs, axis=1), wed_ref[rows, :])


def _experts_dense(x1b, comb, weg_ref, weu_ref, wed_ref):
    moe = _group_experts(x1b, comb, 0, weg_ref, weu_ref, wed_ref)
    for grp in range(1, N_GROUPS):
        moe = moe + _group_experts(x1b, comb, grp, weg_ref, weu_ref, wed_ref)
    return moe


def _experts_sorted(x1b, comb, gi, onehot, tm, weg_ref, weu_ref, wed_ref):
    r = lax.broadcasted_iota(jnp.int32, (tm, tm), 0)
    c = lax.broadcasted_iota(jnp.int32, (tm, tm), 1)
    earlier = jnp.where(r > c, 1.0, 0.0).astype(BF16)
    rank = jnp.sum(onehot * _dot(earlier, onehot.astype(BF16)), axis=-1, keepdims=True)
    key = rank + float(tm) * gi
    key_row = jnp.transpose(jnp.broadcast_to(key, (tm, LANES)))[0:1, :]
    slot = lax.broadcasted_iota(jnp.int32, (MOE_CAP, 1), 0).astype(F32)
    ch, cl = _split(comb)
    moe = jnp.zeros((tm, D_MODEL), F32)
    for grp in range(N_GROUPS):
        sel = jnp.where(key_row == slot + float(tm * grp), 1.0, 0.0).astype(BF16)
        xg = _dot(sel, x1b).astype(BF16)
        cg = _dot(sel, ch) + _dot(sel, cl)
        out = _group_experts(xg, cg, grp, weg_ref, weu_ref, wed_ref)
        moe = moe + _dg(sel, out.astype(BF16), 0, 0)
    return moe


def _mix_kernel(x_ref, lning_ref, lninb_ref, ga_ref, gb_ref, yb_ref, pe_ref, wbout_ref, wo_ref, ln1g_ref,
                ln1b_ref, wr_ref, br_ref, weg_ref, weu_ref, wed_ref, wpg_ref, wpp_ref, ln2g_ref, ln2b_ref,
                o_ref, *scratch, tm, dn_alpha, norm_input, yb_on_lanes):
    x = x_ref[...]
    if norm_input:
        x = _ln(x, lning_ref[...], lninb_ref[...], LN_EPS)
    if yb_on_lanes:
        bo = _dg(yb_ref[...].astype(BF16), wbout_ref[...], 0, 0)
    else:
        bo = _dot(yb_ref[...].astype(BF16), wbout_ref[...])
    merged = ga_ref[...] + gb_ref[...] * bo
    mix = _dot(merged.astype(BF16), wo_ref[...])
    x1 = _ln(dn_alpha * x + mix, ln1g_ref[...], ln1b_ref[...], LN_EPS)
    x1b = x1.astype(BF16)

    x1h, x1l = _split(x1)
    wrh = wr_ref[0]
    wrl = wr_ref[1]
    lg = _dot(x1h, wrh) + (_dot(x1h, wrl) + _dot(x1l, wrh)) + br_ref[...]
    comb, gi = _route(lg, tm)

    if scratch:
        (moe_ref,) = scratch
        lane = lax.broadcasted_iota(jnp.int32, (tm, ROUTER_LANES), 1).astype(F32)
        onehot = jnp.where(lane == gi, 1.0, 0.0)
        fits = jnp.max(jnp.sum(onehot, axis=0, keepdims=True)) <= MOE_CAP

        @pl.when(fits)
        def _():
            moe_ref[...] = _experts_sorted(x1b, comb, gi, onehot, tm, weg_ref, weu_ref, wed_ref)

        @pl.when(jnp.logical_not(fits))
        def _():
            moe_ref[...] = _experts_dense(x1b, comb, weg_ref, weu_ref, wed_ref)

        moe = moe_ref[...]
    else:
        moe = _experts_dense(x1b, comb, weg_ref, weu_ref, wed_ref)

    ple = _sigmoid(_dot(x1b, wpg_ref[...])) * _dot(pe_ref[...].astype(BF16), wpp_ref[...])
    o_ref[...] = _ln(dn_alpha * x1 + moe + ple, ln2g_ref[...], ln2b_ref[...], LN_EPS)


def _mix_call(x, ga, gb, yb, pe, layer, lw, tm, dn_alpha, norm_input, yb_on_lanes):
    m = x.shape[0]
    row = lambda n: pl.BlockSpec((tm, n), lambda i: (i, 0))
    yb_spec = pl.BlockSpec((D_B, tm), lambda i: (0, i)) if yb_on_lanes else row(D_B)
    weights = [lw[n] for n in ("w_b_out", "w_o", "ln1_g", "ln1_b", "w_router", "b_router", "w_e_gate", "w_e_up",
                               "w_e_down", "w_pe_gate", "w_pe_proj", "ln2_g", "ln2_b")]
    return pl.pallas_call(
        functools.partial(_mix_kernel, tm=tm, dn_alpha=dn_alpha, norm_input=norm_input, yb_on_lanes=yb_on_lanes),
        grid=(m // tm,),
        in_specs=[row(D_MODEL), _const_spec((1, D_MODEL)), _const_spec((1, D_MODEL)), row(D_MODEL), row(D_MODEL),
                  yb_spec, pl.BlockSpec((None, tm, PLE_DIM), lambda i: (layer, i, 0))]
                 + [_layer_spec(w, layer) for w in weights],
        out_specs=row(D_MODEL),
        out_shape=jax.ShapeDtypeStruct((m, D_MODEL), F32),
        scratch_shapes=[pltpu.VMEM((tm, D_MODEL), F32)] if tm > MOE_CAP else [],
        compiler_params=pltpu.CompilerParams(dimension_semantics=("parallel",), vmem_limit_bytes=VMEM_LIMIT),
        name="mix_moe",
    )(x, lw["ln_in_g"], lw["ln_in_b"], ga, gb, yb, pe, *weights)


def _prep_weights(p, nb):
    depth = p["w_in"].shape[0]
    vec = lambda z: z.reshape(depth, 1, -1)
    bf = lambda z: z.astype(BF16)
    lanes = lambda z: jnp.broadcast_to(z.reshape(depth, D_B, 1), (depth, D_B, nb))
    zeros64 = jnp.zeros((depth, LORA_W, D_B), F32)
    pad = ROUTER_LANES - N_GROUPS - N_EXPERTS
    w_router = jnp.concatenate(
        [p["w_rg"], jnp.transpose(p["w_re"], (0, 2, 1, 3)).reshape(depth, D_MODEL, N_EXPERTS),
         jnp.zeros((depth, D_MODEL, pad), F32)], axis=2)
    wr_hi = w_router.astype(BF16)
    wr_lo = (w_router - wr_hi.astype(F32)).astype(BF16)
    b_router = jnp.concatenate([p["b_rg"], p["b_re"].reshape(depth, N_EXPERTS), jnp.zeros((depth, pad), F32)], axis=1)
    eye = jnp.eye(CHUNK, dtype=F32)
    w = {n: vec(p[n]) for n in ("b_gate", "ln_v_g", "ln_v_b", "mu_shift", "w0", "a0", "k_k", "k_a", "lnx_g", "lnx_b",
                                "ln1_g", "ln1_b", "ln2_g", "ln2_b")}
    w.update({n: bf(p[n]) for n in ("w_in", "w_a_out", "w_g2", "w_b_out", "w_o", "w_e_gate", "w_e_up", "w_pe_gate",
                                    "w_pe_proj")})
    w.update({
        "w_w2p": bf(jnp.concatenate([p["w_w2"], zeros64], axis=1)),
        "w_a2p": bf(jnp.concatenate([zeros64, p["w_a2"]], axis=1)),
        "r_k": vec(p["r_k"]),
        "lnx_g_t": lanes(p["lnx_g"]), "lnx_b_t": lanes(p["lnx_b"]), "r_k_t": lanes(p["r_k"]),
        "w_router": jnp.stack([wr_hi, wr_lo], axis=1), "b_router": vec(b_router),
        "w_e_down": bf(p["w_e_down"]).reshape(depth, N_EXPERTS * D_EXPERT, D_MODEL),
        "wmix_seq": p["w_s"],
        "bmix_seq": jnp.repeat(jnp.transpose(p["b_s"], (0, 2, 1)), LANES, axis=2),
        "wmix_one": p["w_s"][:, :, 0, 0][:, :, None, None] * eye,
        "bmix_one": jnp.broadcast_to(jnp.repeat(p["b_s"][:, :, 0], LANES, axis=1)[:, None, :], (depth, CHUNK, D_A)),
    })
    return w


def kernel(x_prompt, x_sample, state_rwkv, state_shift, p_prompt, p_sample, ln_in_g, ln_in_b, w_in, b_gate,
           ln_v_g, ln_v_b, w_s, b_s, w_a_out, mu_shift, w0, w_w2, a0, w_a2, w_g2, k_k, k_a, r_k, lnx_g, lnx_b,
           w_b_out, w_o, ln1_g, ln1_b, w_rg, b_rg, w_re, b_re, w_e_gate, w_e_up, w_e_down, w_pe_gate,
           w_pe_proj, ln2_g, ln2_b):
    p = dict(w_in=w_in, b_gate=b_gate, ln_v_g=ln_v_g, ln_v_b=ln_v_b, w_s=w_s, b_s=b_s, w_a_out=w_a_out,
             mu_shift=mu_shift, w0=w0, w_w2=w_w2, a0=a0, w_a2=w_a2, w_g2=w_g2, k_k=k_k, k_a=k_a, r_k=r_k,
             lnx_g=lnx_g, lnx_b=lnx_b, w_b_out=w_b_out, w_o=w_o, ln1_g=ln1_g, ln1_b=ln1_b, w_rg=w_rg, b_rg=b_rg,
             w_re=w_re, b_re=b_re, w_e_gate=w_e_gate, w_e_up=w_e_up, w_e_down=w_e_down,
             w_pe_gate=w_pe_gate, w_pe_proj=w_pe_proj, ln2_g=ln2_g, ln2_b=ln2_b)
    batch, seq, _ = x_prompt.shape
    nb, dec_seq, _ = x_sample.shape
    depth = w_in.shape[0]
    assert dec_seq == 1 and seq % CHUNK == 0 and seq % (RWKV_TILE * RWKV_TILES_PER_STEP) == 0 and nb % LANES == 0
    dn_alpha = (2 * depth) ** 0.25
    tm = 512
    tms = nb
    tiles_per_seq = seq // tm
    last_chunk = ((seq - 1) // CHUNK) * CHUNK

    xp = x_prompt.reshape(batch * seq, D_MODEL)
    xs = x_sample.reshape(nb, D_MODEL)
    pe_p = p_prompt.reshape(depth, batch * seq, PLE_DIM)
    pe_s = p_sample.reshape(depth, nb, PLE_DIM)
    lw = _prep_weights(p, nb)
    lw["ln_in_g"] = ln_in_g.reshape(1, D_MODEL)
    lw["ln_in_b"] = ln_in_b.reshape(1, D_MODEL)
    lw_p = dict(lw, wmix=lw["wmix_seq"], bmix=lw["bmix_seq"])
    lw_s = dict(lw, wmix=lw["wmix_one"], bmix=lw["bmix_one"])
    state_t = jnp.transpose(state_rwkv, (0, 2, 3, 4, 1))
    state_out = jnp.zeros_like(state_t)

    outs = [[] for _ in range(5)]
    for i in range(depth):
        first = i == 0
        res = _proj_call(xp, None, i, lw_p, tm, PROJ_SUB, tiles_per_seq, False, first)
        ga, gb, va, last, ops = res
        yb, s_fin = _rwkv_seq_call(ops, i, lw, batch, seq)
        xp = _mix_call(xp, ga, gb, yb, pe_p, i, lw, tm, dn_alpha, first, False)
        outs[0].append(s_fin)
        outs[1].append(last[tiles_per_seq - 1::tiles_per_seq, 7, :])
        outs[2].append(va.reshape(batch, seq, D_A)[:, last_chunk:])

        res = _proj_call(xs, state_shift, i, lw_s, tms, tms, 1, True, first)
        ga, gb, va, pb, ops_t = res
        yb_t, state_out = _rwkv_step_call(state_t, state_out, i, ops_t, lw, nb)
        xs = _mix_call(xs, ga, gb, yb_t, pe_s, i, lw, tms, dn_alpha, first, True)
        outs[3].append(pb)
        outs[4].append(va.reshape(nb, 1, D_A))

    return (xp.reshape(batch, seq, D_MODEL), xs.reshape(nb, 1, D_MODEL),
            jnp.stack(outs[0]), jnp.stack(outs[1]), jnp.stack(outs[2]),
            jnp.transpose(state_out, (0, 4, 1, 2, 3)), jnp.stack(outs[3]), jnp.stack(outs[4]))
```

```python
import functools

import jax
import jax.numpy as jnp
from jax import lax
from jax.experimental import pallas as pl
from jax.experimental.pallas import tpu as pltpu

F32 = jnp.float32
BF16 = jnp.bfloat16

D_MODEL = 1024
CHUNK = 128
D_A = 512
G_A = 4
D_B = 512
HEAD_B = 64
H_B = D_B // HEAD_B
LORA_W = 64
LORA_A = 64
LORA_G = 128
C_A = 2 * D_A
C_RWKV = 3 * D_B + LORA_W + LORA_A + LORA_G
C_GATE = 2 * D_MODEL
C_IN = C_A + C_RWKV + C_GATE
N_GROUPS = 4
EXPERTS_PER_GROUP = 4
N_EXPERTS = N_GROUPS * EXPERTS_PER_GROUP
D_EXPERT = 256
PLE_DIM = 256
LN_EPS = 1e-5
GN_EPS = 64e-5

LANES = 128
ROUTER_LANES = LANES
RWKV_CHUNK = 64
RWKV_TILE = 256
RWKV_TILES_PER_STEP = 2
OP_R, OP_LW, OP_K, OP_V, OP_KN, OP_IC, OP_G = range(7)
N_OPS = 7
MOE_CAP = 160
PROJ_SUB = 128
VMEM_LIMIT = 56 * 1024 * 1024


def _ln(x, g, b, eps):
    mu = jnp.mean(x, axis=-1, keepdims=True)
    xc = x - mu
    var = jnp.mean(xc * xc, axis=-1, keepdims=True)
    return xc * lax.rsqrt(var + eps) * g + b


def _dot(a, b):
    return jnp.dot(a, b, preferred_element_type=F32)


def _split(x):
    hi = x.astype(BF16)
    lo = (x - hi.astype(F32)).astype(BF16)
    return hi, lo


def _dg(a, b, ca, cb):
    return lax.dot_general(a, b, (((ca,), (cb,)), ((), ())), preferred_element_type=F32)


def _gelu_tanh(x):
    return 0.5 * x * (1.0 + jnp.tanh(0.7978845608028654 * (x + 0.044715 * (x * x * x))))


def _sigmoid(x):
    return 1.0 / (1.0 + jnp.exp(-x))


def _softplus(x):
    return jnp.maximum(x, 0.0) + jnp.log(1.0 + jnp.exp(-jnp.abs(x)))


def _pair_head_sum(z):
    in_first = lax.broadcasted_iota(jnp.int32, (1, LANES), 1) < HEAD_B
    s0 = jnp.sum(jnp.where(in_first, z, 0.0), axis=-1, keepdims=True)
    s1 = jnp.sum(jnp.where(in_first, 0.0, z), axis=-1, keepdims=True)
    return jnp.where(in_first, s0, s1)


def _const_spec(shape):
    nd = len(shape)
    return pl.BlockSpec(shape, lambda *_: (0,) * nd, pipeline_mode=pl.Buffered(1))


def _layer_spec(arr, layer):
    nd = arr.ndim - 1
    return pl.BlockSpec((None,) + arr.shape[1:], lambda *_: (layer,) + (0,) * nd, pipeline_mode=pl.Buffered(1))


def _proj_kernel(*refs, tm, sub, tiles_per_seq, single_token, norm_input):
    if single_token:
        (x_ref, lning_ref, lninb_ref, prev_ref, win_ref, bgate_ref, lnvg_ref, lnvb_ref, wmix_ref, bmix_ref,
         waout_ref, mu_ref, w0_ref, ww2_ref, a0_ref, wa2_ref, wg2_ref, kkw_ref, kaw_ref,
         ga_ref, gb_ref, va_ref, last_ref, ops_ref) = refs
    else:
        (x_ref, lning_ref, lninb_ref, win_ref, bgate_ref, lnvg_ref, lnvb_ref, wmix_ref, bmix_ref,
         waout_ref, mu_ref, w0_ref, ww2_ref, a0_ref, wa2_ref, wg2_ref, kkw_ref, kaw_ref,
         ga_ref, gb_ref, va_ref, last_ref, ops_ref, carry_ref) = refs

    n_sub = tm // sub
    rr = lax.broadcasted_iota(jnp.int32, (CHUNK, CHUNK), 0)
    cc = lax.broadcasted_iota(jnp.int32, (CHUNK, CHUNK), 1)
    wmix = [jnp.where(rr >= cc, wmix_ref[g], 0.0).astype(BF16) for g in range(G_A)]
    if not single_token:
        i = pl.program_id(0)

        @pl.when(i == 0)
        def _():
            carry_ref[...] = jnp.zeros_like(carry_ref)

        first = jnp.where(i % tiles_per_seq == 0, jnp.zeros((1, C_RWKV), F32), carry_ref[0:1, :])
        row = lax.broadcasted_iota(jnp.int32, (sub, C_RWKV), 0)

    def project(q):
        x = x_ref[q * sub:(q + 1) * sub, :]
        if norm_input:
            x = _ln(x, lning_ref[...], lninb_ref[...], LN_EPS)
        xb = x.astype(BF16)
        return (_dot(xb, win_ref[:, 0:C_A]), _dot(xb, win_ref[:, C_A + C_RWKV:]),
                _dot(xb, win_ref[:, C_A:C_A + C_RWKV]))

    def finish(q, pa, pg, pb, prev_row):
        rows = slice(q * sub, (q + 1) * sub)
        z = _gelu_tanh(pa)
        u = z[:, :D_A]
        va = _ln(z[:, D_A:], lnvg_ref[...], lnvb_ref[...], LN_EPS)
        va_ref[rows, :] = va
        vab = va.astype(BF16)
        mix = jnp.concatenate(
            [jnp.concatenate([_dot(wmix[g], vab[c * CHUNK:(c + 1) * CHUNK, g * LANES:(g + 1) * LANES])
                              for g in range(G_A)], axis=1) + bmix_ref[...]
             for c in range(sub // CHUNK)], axis=0)
        ao = _dot((u * mix).astype(BF16), waout_ref[...])
        gates = _sigmoid(pg + bgate_ref[...])
        ga_ref[rows, :] = (gates[:, :D_MODEL] * ao).astype(BF16)
        gb_ref[rows, :] = gates[:, D_MODEL:].astype(BF16)

        if single_token:
            prev = prev_ref[rows, :]
        else:
            prev = jnp.where(row == 0, prev_row, pltpu.roll(pb, 1, 0))
        xs = pb + mu_ref[...] * (prev - pb)
        r = xs[:, 0:D_B]
        k = xs[:, D_B:2 * D_B]
        v = xs[:, 2 * D_B:3 * D_B]
        o = 3 * D_B
        wa_d = xs[:, o:o + LORA_W + LORA_A]
        gd = xs[:, o + LORA_W + LORA_A:]
        w_log = -_softplus(-(w0_ref[...] + _dot(jnp.tanh(wa_d).astype(BF16), ww2_ref[...]))) - 0.5
        iclr = _sigmoid(a0_ref[...] + _dot(wa_d.astype(BF16), wa2_ref[...]))
        kk = k * kkw_ref[...]
        sq = kk * kk
        ss = jnp.concatenate([_pair_head_sum(sq[:, p * LANES:(p + 1) * LANES]) for p in range(D_B // LANES)], axis=1)
        packed = jnp.concatenate(
            [r, -jnp.exp(w_log), k * (1.0 + (iclr - 1.0) * kaw_ref[...]), v,
             kk / jnp.maximum(jnp.sqrt(ss), 1e-12), iclr, _dot(_sigmoid(gd).astype(BF16), wg2_ref[...])], axis=1)
        if single_token:
            ops_ref[:, rows] = packed.T
        else:
            ops_ref[rows, :] = packed

    proj = [project(0)]
    for q in range(n_sub):
        if q + 1 < n_sub:
            proj.append(project(q + 1))
        prev_row = None
        if not single_token:
            prev_row = first if q == 0 else proj[q - 1][2][sub - 1:sub, :]
        finish(q, *proj[q], prev_row)
    pb_last = proj[n_sub - 1][2]
    if single_token:
        last_ref[...] = pb_last
    else:
        carry_ref[0:1, :] = pb_last[sub - 1:sub, :]
        last_ref[0] = pb_last[sub - 8:sub, :]


def _proj_call(x, prev, layer, lw, tm, sub, tiles_per_seq, single_token, norm_input):
    m = x.shape[0]
    nt = m // tm
    row = lambda n: pl.BlockSpec((tm, n), lambda i: (i, 0))
    in_specs = [row(D_MODEL), _const_spec((1, D_MODEL)), _const_spec((1, D_MODEL))]
    args = [x, lw["ln_in_g"], lw["ln_in_b"]]
    if single_token:
        in_specs.append(pl.BlockSpec((None, tm, C_RWKV), lambda i: (layer, i, 0)))
        args.append(prev)
    weights = [lw[n] for n in ("w_in", "b_gate", "ln_v_g", "ln_v_b", "wmix", "bmix", "w_a_out", "mu_shift", "w0",
                               "w_w2p", "a0", "w_a2p", "w_g2", "k_k", "k_a")]
    in_specs += [_layer_spec(w, layer) for w in weights]
    args += weights
    out_shape = [jax.ShapeDtypeStruct((m, D_MODEL), BF16), jax.ShapeDtypeStruct((m, D_MODEL), BF16),
                 jax.ShapeDtypeStruct((m, D_A), F32)]
    out_specs = [row(D_MODEL), row(D_MODEL), row(D_A)]
    if single_token:
        out_shape.append(jax.ShapeDtypeStruct((m, C_RWKV), F32))
        out_specs.append(row(C_RWKV))
    else:
        out_shape.append(jax.ShapeDtypeStruct((nt, 8, C_RWKV), F32))
        out_specs.append(pl.BlockSpec((1, 8, C_RWKV), lambda i: (i, 0, 0)))
    if single_token:
        out_shape.append(jax.ShapeDtypeStruct((N_OPS * D_B, m), F32))
        out_specs.append(pl.BlockSpec((N_OPS * D_B, tm), lambda i: (0, i)))
    else:
        out_shape.append(jax.ShapeDtypeStruct((m, N_OPS * D_B), F32))
        out_specs.append(row(N_OPS * D_B))
    scratch = [] if single_token else [pltpu.VMEM((8, C_RWKV), F32)]
    return pl.pallas_call(
        functools.partial(_proj_kernel, tm=tm, sub=sub, tiles_per_seq=tiles_per_seq, single_token=single_token,
                          norm_input=norm_input),
        grid=(nt,),
        in_specs=in_specs,
        out_specs=out_specs,
        out_shape=out_shape,
        scratch_shapes=scratch,
        compiler_params=pltpu.CompilerParams(dimension_semantics=("arbitrary",), vmem_limit_bytes=VMEM_LIMIT),
        name="proj_single" if single_token else "proj_seq",
    )(*args)


def _rwkv_tile_kernel(ops_ref, lnxg_ref, lnxb_ref, rkw_ref, y_ref, sfin_ref, state_ref, *, n_tile):
    ti = pl.program_id(1)
    n_t = pl.num_programs(1)
    C = RWKV_CHUNK
    RT = RWKV_TILE
    n_ch = RT // C
    n_pair = H_B // 2

    @pl.when(ti == 0)
    def _():
        state_ref[...] = jnp.zeros_like(state_ref)

    row = lax.broadcasted_iota(jnp.int32, (RT, RT), 0)
    col = lax.broadcasted_iota(jnp.int32, (RT, RT), 1)
    same = (row // C) == (col // C)
    incl = jnp.logical_and(same, row >= col)
    strict = jnp.logical_and(same, row > col)
    row_in_chunk = lax.broadcasted_iota(jnp.int32, (RT, 1), 0) % C
    eye = jnp.where(row == col, 1.0, 0.0)
    prow = lax.broadcasted_iota(jnp.int32, (LANES, LANES), 0)
    pcol = lax.broadcasted_iota(jnp.int32, (LANES, LANES), 1)
    pair_bd = (prow // HEAD_B) == (pcol // HEAD_B)
    in_first = lax.broadcasted_iota(jnp.int32, (1, LANES), 1) < HEAD_B
    first_head = jnp.where(in_first, 1.0, 0.0)
    second_head = 1.0 - first_head
    sls = [slice(p * LANES, (p + 1) * LANES) for p in range(n_pair)]
    pr = [h // 2 for h in range(H_B)]

    def op(j, rows, sl):
        return ops_ref[rows, j * D_B + sl.start:j * D_B + sl.stop]

    def prologue(t):
        rows = slice(t * RT, (t + 1) * RT)
        lw = op(OP_LW, rows, slice(0, D_B))
        cum = lw
        for shift in (1, 2, 4, 8, 16, 32):
            cum = cum + jnp.where(row_in_chunk >= shift, pltpu.roll(cum, shift, 0), 0.0)
        ends = [cum[(c + 1) * C - 1:(c + 1) * C, :] for c in range(n_ch)]
        cum_end = jnp.concatenate([jnp.broadcast_to(e, (C, D_B)) for e in ends], axis=0)
        d = dict(rows=rows, ends=ends, atm=[], rtm=[], rt=[], bt=[], kt=[], bh=[], kh=[], v=[])
        for p in range(n_pair):
            sl = sls[p]
            cum_p = cum[:, sl]
            kn_p = op(OP_KN, rows, sl)
            b_p = kn_p * op(OP_IC, rows, sl)
            k_p = op(OP_K, rows, sl)
            e_inv = jnp.exp(-cum_p)
            e_hat = jnp.exp(cum_end[:, sl] - cum_p)
            at_p = -kn_p * jnp.exp(cum_p - lw[:, sl])
            rt_p = op(OP_R, rows, sl) * jnp.exp(cum_p)
            for hm in (first_head, second_head):
                d["atm"].append((at_p * hm).astype(BF16))
                d["rtm"].append((rt_p * hm).astype(BF16))
            d["rt"].append(rt_p)
            d["bt"].append((b_p * e_inv).astype(BF16))
            d["kt"].append((k_p * e_inv).astype(BF16))
            d["bh"].append((b_p * e_hat).astype(BF16))
            d["kh"].append((k_p * e_hat).astype(BF16))
            d["v"].append(op(OP_V, rows, sl).astype(BF16))
        return d

    def main(d):
        atm, rtm, bt, kt, v = d["atm"], d["rtm"], d["bt"], d["kt"], d["v"]
        xs = [jnp.where(strict, _dg(atm[h], bt[pr[h]], 1, 1), 0.0) for h in range(H_B)]
        yield
        ts = [eye + x for x in xs]
        xb = [x.astype(BF16) for x in xs]
        for _ in range(5):
            xb = [_dot(z, z).astype(BF16) for z in xb]
            yield
            ts = [t + _dot(t.astype(BF16), z) for t, z in zip(ts, xb)]
            yield
        tb = [t.astype(BF16) for t in ts]
        a_ak = [jnp.where(strict, _dg(atm[h], kt[pr[h]], 1, 1), 0.0).astype(BF16) for h in range(H_B)]
        yield
        av = [_dot(a_ak[h], v[pr[h]]).astype(BF16) for h in range(H_B)]
        yield
        tx = [_dot(tb[h], jnp.concatenate([av[h], atm[h]], axis=1)) for h in range(H_B)]
        yield
        a_rb = [jnp.where(incl, _dg(rtm[h], bt[pr[h]], 1, 1), 0.0).astype(BF16) for h in range(H_B)]
        yield
        a_rk = [jnp.where(incl, _dg(rtm[h], kt[pr[h]], 1, 1), 0.0).astype(BF16) for h in range(H_B)]
        yield
        rb = [_dot(a_rb[h], tx[h].astype(BF16)) for h in range(H_B)]
        yield
        rkv = [_dot(a_rk[h], v[pr[h]]) for h in range(H_B)]
        d["w2"] = [jnp.where(in_first, tx[2 * p][:, :LANES], tx[2 * p + 1][:, :LANES]).astype(BF16)
                   for p in range(n_pair)]
        d["a_t"] = [(tx[2 * p][:, LANES:] + tx[2 * p + 1][:, LANES:]).astype(BF16) for p in range(n_pair)]
        d["rhat"] = [(d["rt"][p] + (rb[2 * p][:, LANES:] + rb[2 * p + 1][:, LANES:])).astype(BF16)
                     for p in range(n_pair)]
        d["y0"] = [jnp.where(in_first, rb[2 * p][:, :LANES] + rkv[2 * p], rb[2 * p + 1][:, :LANES] + rkv[2 * p + 1])
                   for p in range(n_pair)]
        yield

    def tail(d, st):
        w2, a_t, rhat, y0, bh, kh, v = d["w2"], d["a_t"], d["rhat"], d["y0"], d["bh"], d["kh"], d["v"]
        ys = [[] for _ in range(n_pair)]
        for c in range(n_ch):
            rows = slice(c * C, (c + 1) * C)
            el = [jnp.where(pair_bd, _dg(a_t[p][rows], bh[p][rows], 0, 0), 0.0).astype(BF16) for p in range(n_pair)]
            gm = [jnp.where(pair_bd,
                            _dg(jnp.concatenate([w2[p][rows], v[p][rows]], axis=0),
                                jnp.concatenate([bh[p][rows], kh[p][rows]], axis=0), 0, 0), 0.0)
                  for p in range(n_pair)]
            s = st["s"]
            sb = [z.astype(BF16) for z in s]
            for p in range(n_pair):
                ys[p].append(_dg(rhat[p][rows], sb[p], 1, 1) + y0[p][rows])
            st["s"] = [s[p] * jnp.exp(d["ends"][c][:, sls[p]]) + _dot(sb[p], el[p]) + gm[p] for p in range(n_pair)]
            yield
        rows = d["rows"]
        for p in range(n_pair):
            sl = sls[p]
            y = jnp.concatenate(ys[p], axis=0)
            mu = _pair_head_sum(y) * (1.0 / HEAD_B)
            yc = y - mu
            var = _pair_head_sum(yc * yc) * (1.0 / HEAD_B)
            yn = yc * lax.rsqrt(var + GN_EPS) * lnxg_ref[:, sl] + lnxb_ref[:, sl]
            bonus = (_pair_head_sum(op(OP_R, rows, sl) * op(OP_K, rows, sl) * rkw_ref[:, sl])
                     * op(OP_V, rows, sl))
            y_ref[rows, sl] = (yn + bonus) * op(OP_G, rows, sl)
            yield

    def drain(gen):
        for _ in gen:
            pass

    st = {"s": [state_ref[p] for p in range(n_pair)]}
    pro = prologue(0)
    stages = main(pro)
    next(stages)
    prev_tail = None
    for t in range(n_tile):
        nxt = prologue(t + 1) if t + 1 < n_tile else None
        for n, _ in enumerate(stages):
            if prev_tail is not None and n % 2 == 1:
                next(prev_tail, None)
        if prev_tail is not None:
            drain(prev_tail)
        prev_tail = tail(pro, st)
        if nxt is not None:
            pro = nxt
            stages = main(pro)
            next(stages)
    drain(prev_tail)
    for p in range(n_pair):
        state_ref[p] = st["s"][p]

    @pl.when(ti == n_t - 1)
    def _():
        sfin_ref[0] = state_ref[...]


def _rwkv_seq_call(ops, layer, lw, batch, seq):
    RT = RWKV_TILE * RWKV_TILES_PER_STEP
    n_t = seq // RT
    n_pair = H_B // 2
    blk = pl.BlockSpec((RT, D_B), lambda b, t: (b * n_t + t, 0))
    vec = pl.BlockSpec((None, 1, D_B), lambda b, t: (layer, 0, 0))
    y, s_bd = pl.pallas_call(
        functools.partial(_rwkv_tile_kernel, n_tile=RWKV_TILES_PER_STEP),
        grid=(batch, n_t),
        in_specs=[pl.BlockSpec((RT, N_OPS * D_B), lambda b, t: (b * n_t + t, 0))] + [vec] * 3,
        out_specs=[blk, pl.BlockSpec((1, n_pair, LANES, LANES), lambda b, t: (b, 0, 0, 0))],
        out_shape=[jax.ShapeDtypeStruct((batch * seq, D_B), F32),
                   jax.ShapeDtypeStruct((batch, n_pair, LANES, LANES), F32)],
        scratch_shapes=[pltpu.VMEM((n_pair, LANES, LANES), F32)],
        compiler_params=pltpu.CompilerParams(dimension_semantics=("arbitrary", "arbitrary"),
                                             vmem_limit_bytes=VMEM_LIMIT),
        name="rwkv_seq",
    )(ops, lw["lnx_g"], lw["lnx_b"], lw["r_k"])
    s_bd = s_bd.reshape(batch, n_pair, 2, HEAD_B, 2, HEAD_B)
    s_fin = jnp.stack([s_bd[:, :, 0, :, 0, :], s_bd[:, :, 1, :, 1, :]], axis=2)
    return y, s_fin.reshape(batch, H_B, HEAD_B, HEAD_B)


def _rwkv_step_kernel(s_ref, r_ref, lw_ref, k_ref, v_ref, kn_ref, ic_ref, g_ref, lnxg_ref, lnxb_ref, rkw_ref,
                      buf_ref, y_ref, so_ref, y_scr):
    del buf_ref
    kn = kn_ref[...]
    a = -kn
    b = kn * ic_ref[...]
    w = jnp.exp(lw_ref[...])
    k = k_ref[...]
    r = r_ref[...]
    for i in range(HEAD_B):
        s = s_ref[i]
        sa = jnp.sum(s * a, axis=0, keepdims=True)
        s1 = s * w + sa * b + v_ref[i:i + 1, :] * k
        so_ref[i] = s1
        y_scr[i:i + 1, :] = jnp.sum(s1 * r, axis=0, keepdims=True)
    y = y_scr[...]
    mu = jnp.mean(y, axis=0, keepdims=True)
    yc = y - mu
    var = jnp.mean(yc * yc, axis=0, keepdims=True)
    yn = yc * lax.rsqrt(var + GN_EPS) * lnxg_ref[...] + lnxb_ref[...]
    bonus = jnp.sum(r * k * rkw_ref[...], axis=0, keepdims=True) * v_ref[...]
    y_ref[...] = (yn + bonus) * g_ref[...]


def _rwkv_step_call(state_t, buf, layer, ops_t, lw, nb):
    sblk = pl.BlockSpec((None, None, HEAD_B, HEAD_B, nb), lambda h: (layer, h, 0, 0, 0))
    vblk = pl.BlockSpec((HEAD_B, nb), lambda h: (h, 0))
    pblk = pl.BlockSpec((None, HEAD_B, nb), lambda h: (layer, h, 0))
    return pl.pallas_call(
        _rwkv_step_kernel,
        grid=(H_B,),
        in_specs=[sblk] + [pl.BlockSpec((HEAD_B, nb), lambda h, j=j: (j * H_B + h, 0)) for j in range(N_OPS)]
                 + [pblk] * 3 + [pl.BlockSpec(memory_space=pl.ANY)],
        out_specs=[vblk, sblk],
        out_shape=[jax.ShapeDtypeStruct((D_B, nb), F32), jax.ShapeDtypeStruct(buf.shape, F32)],
        scratch_shapes=[pltpu.VMEM((HEAD_B, nb), F32)],
        input_output_aliases={11: 1},
        compiler_params=pltpu.CompilerParams(dimension_semantics=("arbitrary",)),
        name="rwkv_step",
    )(state_t, *([ops_t] * N_OPS), lw["lnx_g_t"], lw["lnx_b_t"], lw["r_k_t"], buf)


def _route(lg, tm):
    lane = lax.broadcasted_iota(jnp.int32, (tm, ROUTER_LANES), 1)
    lanef = lane.astype(F32)
    big = jnp.float32(1e9)
    ninf = jnp.float32(-jnp.inf)
    isg = lane < N_GROUPS
    lgm = jnp.where(isg, lg, ninf)
    gmax = jnp.max(lgm, axis=-1, keepdims=True)
    den = jnp.sum(jnp.exp(lgm - gmax), axis=-1, keepdims=True)
    gp = 1.0 / den
    gi = jnp.min(jnp.where(lgm == gmax, lanef, big), axis=-1, keepdims=True)
    grp = ((lane - N_GROUPS) >> 2).astype(F32)
    ise = jnp.where(lane >= N_GROUPS, grp, big) == gi
    le = jnp.where(ise, lg, ninf)
    m1 = jnp.max(le, axis=-1, keepdims=True)
    i1 = jnp.min(jnp.where(le == m1, lanef, big), axis=-1, keepdims=True)
    le2 = jnp.where(lanef == i1, ninf, le)
    m2 = jnp.max(le2, axis=-1, keepdims=True)
    i2 = jnp.min(jnp.where(le2 == m2, lanef, big), axis=-1, keepdims=True)
    e2 = jnp.exp(m2 - m1)
    w1 = 1.0 / (1.0 + e2)
    w2 = e2 / (1.0 + e2)
    return jnp.where(lanef == i1, w1 * gp, 0.0) + jnp.where(lanef == i2, w2 * gp, 0.0), gi


def _group_experts(xb, cw, grp, weg_ref, weu_ref, wed_ref):
    hs = []
    for j in range(EXPERTS_PER_GROUP):
        e = grp * EXPERTS_PER_GROUP + j
        hg = _dot(xb, weg_ref[e])
        hu = _dot(xb, weu_ref[e])
        ce = cw[:, N_GROUPS + e:N_GROUPS + e + 1]
        hs.append(((hg * _sigmoid(hg)) * hu * ce).astype(BF16))
    rows = slice(grp * EXPERTS_PER_GROUP * D_EXPERT, (grp + 1) * EXPERTS_PER_GROUP * D_EXPERT)
    return _dot(jnp.concatenate(hs, axis=1), wed_ref[rows, :])


def _experts_dense(x1b, comb, weg_ref, weu_ref, wed_ref):
    moe = _group_experts(x1b, comb, 0, weg_ref, weu_ref, wed_ref)
    for grp in range(1, N_GROUPS):
        moe = moe + _group_experts(x1b, comb, grp, weg_ref, weu_ref, wed_ref)
    return moe


def _experts_sorted(x1b, comb, gi, onehot, tm, weg_ref, weu_ref, wed_ref):
    r = lax.broadcasted_iota(jnp.int32, (tm, tm), 0)
    c = lax.broadcasted_iota(jnp.int32, (tm, tm), 1)
    earlier = jnp.where(r > c, 1.0, 0.0).astype(BF16)
    rank = jnp.sum(onehot * _dot(earlier, onehot.astype(BF16)), axis=-1, keepdims=True)
    key = rank + float(tm) * gi
    key_row = jnp.transpose(jnp.broadcast_to(key, (tm, LANES)))[0:1, :]
    slot = lax.broadcasted_iota(jnp.int32, (MOE_CAP, 1), 0).astype(F32)
    ch, cl = _split(comb)
    moe = jnp.zeros((tm, D_MODEL), F32)
    for grp in range(N_GROUPS):
        sel = jnp.where(key_row == slot + float(tm * grp), 1.0, 0.0).astype(BF16)
        xg = _dot(sel, x1b).astype(BF16)
        cg = _dot(sel, ch) + _dot(sel, cl)
        out = _group_experts(xg, cg, grp, weg_ref, weu_ref, wed_ref)
        moe = moe + _dg(sel, out.astype(BF16), 0, 0)
    return moe


def _mix_kernel(x_ref, lning_ref, lninb_ref, ga_ref, gb_ref, yb_ref, pe_ref, wbout_ref, wo_ref, ln1g_ref,
                ln1b_ref, wr_ref, br_ref, weg_ref, weu_ref, wed_ref, wpg_ref, wpp_ref, ln2g_ref, ln2b_ref,
                o_ref, *scratch, tm, dn_alpha, norm_input, yb_on_lanes):
    x = x_ref[...]
    if norm_input:
        x = _ln(x, lning_ref[...], lninb_ref[...], LN_EPS)
    if yb_on_lanes:
        bo = _dg(yb_ref[...].astype(BF16), wbout_ref[...], 0, 0)
    else:
        bo = _dot(yb_ref[...].astype(BF16), wbout_ref[...])
    merged = ga_ref[...] + gb_ref[...] * bo
    mix = _dot(merged.astype(BF16), wo_ref[...])
    x1 = _ln(dn_alpha * x + mix, ln1g_ref[...], ln1b_ref[...], LN_EPS)
    x1b = x1.astype(BF16)

    x1h, x1l = _split(x1)
    wrh = wr_ref[0]
    wrl = wr_ref[1]
    lg = _dot(x1h, wrh) + (_dot(x1h, wrl) + _dot(x1l, wrh)) + br_ref[...]
    comb, gi = _route(lg, tm)

    if scratch:
        (moe_ref,) = scratch
        lane = lax.broadcasted_iota(jnp.int32, (tm, ROUTER_LANES), 1).astype(F32)
        onehot = jnp.where(lane == gi, 1.0, 0.0)
        fits = jnp.max(jnp.sum(onehot, axis=0, keepdims=True)) <= MOE_CAP

        @pl.when(fits)
        def _():
            moe_ref[...] = _experts_sorted(x1b, comb, gi, onehot, tm, weg_ref, weu_ref, wed_ref)

        @pl.when(jnp.logical_not(fits))
        def _():
            moe_ref[...] = _experts_dense(x1b, comb, weg_ref, weu_ref, wed_ref)

        moe = moe_ref[...]
    else:
        moe = _experts_dense(x1b, comb, weg_ref, weu_ref, wed_ref)

    ple = _sigmoid(_dot(x1b, wpg_ref[...])) * _dot(pe_ref[...].astype(BF16), wpp_ref[...])
    o_ref[...] = _ln(dn_alpha * x1 + moe + ple, ln2g_ref[...], ln2b_ref[...], LN_EPS)


def _mix_call(x, ga, gb, yb, pe, layer, lw, tm, dn_alpha, norm_input, yb_on_lanes):
    m = x.shape[0]
    row = lambda n: pl.BlockSpec((tm, n), lambda i: (i, 0))
    yb_spec = pl.BlockSpec((D_B, tm), lambda i: (0, i)) if yb_on_lanes else row(D_B)
    weights = [lw[n] for n in ("w_b_out", "w_o", "ln1_g", "ln1_b", "w_router", "b_router", "w_e_gate", "w_e_up",
                               "w_e_down", "w_pe_gate", "w_pe_proj", "ln2_g", "ln2_b")]
    return pl.pallas_call(
        functools.partial(_mix_kernel, tm=tm, dn_alpha=dn_alpha, norm_input=norm_input, yb_on_lanes=yb_on_lanes),
        grid=(m // tm,),
        in_specs=[row(D_MODEL), _const_spec((1, D_MODEL)), _const_spec((1, D_MODEL)), row(D_MODEL), row(D_MODEL),
                  yb_spec, pl.BlockSpec((None, tm, PLE_DIM), lambda i: (layer, i, 0))]
                 + [_layer_spec(w, layer) for w in weights],
        out_specs=row(D_MODEL),
        out_shape=jax.ShapeDtypeStruct((m, D_MODEL), F32),
        scratch_shapes=[pltpu.VMEM((tm, D_MODEL), F32)] if tm > MOE_CAP else [],
        compiler_params=pltpu.CompilerParams(dimension_semantics=("parallel",), vmem_limit_bytes=VMEM_LIMIT),
        name="mix_moe",
    )(x, lw["ln_in_g"], lw["ln_in_b"], ga, gb, yb, pe, *weights)


def _prep_weights(p, nb):
    depth = p["w_in"].shape[0]
    vec = lambda z: z.reshape(depth, 1, -1)
    bf = lambda z: z.astype(BF16)
    lanes = lambda z: jnp.broadcast_to(z.reshape(depth, D_B, 1), (depth, D_B, nb))
    zeros64 = jnp.zeros((depth, LORA_W, D_B), F32)
    pad = ROUTER_LANES - N_GROUPS - N_EXPERTS
    w_router = jnp.concatenate(
        [p["w_rg"], jnp.transpose(p["w_re"], (0, 2, 1, 3)).reshape(depth, D_MODEL, N_EXPERTS),
         jnp.zeros((depth, D_MODEL, pad), F32)], axis=2)
    wr_hi = w_router.astype(BF16)
    wr_lo = (w_router - wr_hi.astype(F32)).astype(BF16)
    b_router = jnp.concatenate([p["b_rg"], p["b_re"].reshape(depth, N_EXPERTS), jnp.zeros((depth, pad), F32)], axis=1)
    eye = jnp.eye(CHUNK, dtype=F32)
    w = {n: vec(p[n]) for n in ("b_gate", "ln_v_g", "ln_v_b", "mu_shift", "w0", "a0", "k_k", "k_a", "lnx_g", "lnx_b",
                                "ln1_g", "ln1_b", "ln2_g", "ln2_b")}
    w.update({n: bf(p[n]) for n in ("w_in", "w_a_out", "w_g2", "w_b_out", "w_o", "w_e_gate", "w_e_up", "w_pe_gate",
                                    "w_pe_proj")})
    w.update({
        "w_w2p": bf(jnp.concatenate([p["w_w2"], zeros64], axis=1)),
        "w_a2p": bf(jnp.concatenate([zeros64, p["w_a2"]], axis=1)),
        "r_k": vec(p["r_k"]),
        "lnx_g_t": lanes(p["lnx_g"]), "lnx_b_t": lanes(p["lnx_b"]), "r_k_t": lanes(p["r_k"]),
        "w_router": jnp.stack([wr_hi, wr_lo], axis=1), "b_router": vec(b_router),
        "w_e_down": bf(p["w_e_down"]).reshape(depth, N_EXPERTS * D_EXPERT, D_MODEL),
        "wmix_seq": p["w_s"],
        "bmix_seq": jnp.repeat(jnp.transpose(p["b_s"], (0, 2, 1)), LANES, axis=2),
        "wmix_one": p["w_s"][:, :, 0, 0][:, :, None, None] * eye,
        "bmix_one": jnp.broadcast_to(jnp.repeat(p["b_s"][:, :, 0], LANES, axis=1)[:, None, :], (depth, CHUNK, D_A)),
    })
    return w


def kernel(x_prompt, x_sample, state_rwkv, state_shift, p_prompt, p_sample, ln_in_g, ln_in_b, w_in, b_gate,
           ln_v_g, ln_v_b, w_s, b_s, w_a_out, mu_shift, w0, w_w2, a0, w_a2, w_g2, k_k, k_a, r_k, lnx_g, lnx_b,
           w_b_out, w_o, ln1_g, ln1_b, w_rg, b_rg, w_re, b_re, w_e_gate, w_e_up, w_e_down, w_pe_gate,
           w_pe_proj, ln2_g, ln2_b):
    p = dict(w_in=w_in, b_gate=b_gate, ln_v_g=ln_v_g, ln_v_b=ln_v_b, w_s=w_s, b_s=b_s, w_a_out=w_a_out,
             mu_shift=mu_shift, w0=w0, w_w2=w_w2, a0=a0, w_a2=w_a2, w_g2=w_g2, k_k=k_k, k_a=k_a, r_k=r_k,
             lnx_g=lnx_g, lnx_b=lnx_b, w_b_out=w_b_out, w_o=w_o, ln1_g=ln1_g, ln1_b=ln1_b, w_rg=w_rg, b_rg=b_rg,
             w_re=w_re, b_re=b_re, w_e_gate=w_e_gate, w_e_up=w_e_up, w_e_down=w_e_down,
             w_pe_gate=w_pe_gate, w_pe_proj=w_pe_proj, ln2_g=ln2_g, ln2_b=ln2_b)
    batch, seq, _ = x_prompt.shape
    nb, dec_seq, _ = x_sample.shape
    depth = w_in.shape[0]
    assert dec_seq == 1 and seq % CHUNK == 0 and seq % (RWKV_TILE * RWKV_TILES_PER_STEP) == 0 and nb % LANES == 0
    dn_alpha = (2 * depth) ** 0.25
    tm = 512
    tms = nb
    tiles_per_seq = seq // tm
    last_chunk = ((seq - 1) // CHUNK) * CHUNK

    xp = x_prompt.reshape(batch * seq, D_MODEL)
    xs = x_sample.reshape(nb, D_MODEL)
    pe_p = p_prompt.reshape(depth, batch * seq, PLE_DIM)
    pe_s = p_sample.reshape(depth, nb, PLE_DIM)
    lw = _prep_weights(p, nb)
    lw["ln_in_g"] = ln_in_g.reshape(1, D_MODEL)
    lw["ln_in_b"] = ln_in_b.reshape(1, D_MODEL)
    lw_p = dict(lw, wmix=lw["wmix_seq"], bmix=lw["bmix_seq"])
    lw_s = dict(lw, wmix=lw["wmix_one"], bmix=lw["bmix_one"])
    state_t = jnp.transpose(state_rwkv, (0, 2, 3, 4, 1))
    state_out = jnp.zeros_like(state_t)

    outs = [[] for _ in range(5)]
    for i in range(depth):
        first = i == 0
        res = _proj_call(xp, None, i, lw_p, tm, PROJ_SUB, tiles_per_seq, False, first)
        ga, gb, va, last, ops = res
        yb, s_fin = _rwkv_seq_call(ops, i, lw, batch, seq)
        xp = _mix_call(xp, ga, gb, yb, pe_p, i, lw, tm, dn_alpha, first, False)
        outs[0].append(s_fin)
        outs[1].append(last[tiles_per_seq - 1::tiles_per_seq, 7, :])
        outs[2].append(va.reshape(batch, seq, D_A)[:, last_chunk:])

        res = _proj_call(xs, state_shift, i, lw_s, tms, tms, 1, True, first)
        ga, gb, va, pb, ops_t = res
        yb_t, state_out = _rwkv_step_call(state_t, state_out, i, ops_t, lw, nb)
        xs = _mix_call(xs, ga, gb, yb_t, pe_s, i, lw, tms, dn_alpha, first, True)
        outs[3].append(pb)
        outs[4].append(va.reshape(nb, 1, D_A))

    return (xp.reshape(batch, seq, D_MODEL), xs.reshape(nb, 1, D_MODEL),
            jnp.stack(outs[0]), jnp.stack(outs[1]), jnp.stack(outs[2]),
            jnp.transpose(state_out, (0, 4, 1, 2, 3)), jnp.stack(outs[3]), jnp.stack(outs[4]))
```

```python
import functools

import jax
import jax.numpy as jnp
from jax import lax
from jax.experimental import pallas as pl
from jax.experimental.pallas import tpu as pltpu

F32 = jnp.float32
BF16 = jnp.bfloat16

D_MODEL = 1024
CHUNK = 128
D_A = 512
G_A = 4
D_B = 512
HEAD_B = 64
H_B = D_B // HEAD_B
LORA_W = 64
LORA_A = 64
LORA_G = 128
C_A = 2 * D_A
C_RWKV = 3 * D_B + LORA_W + LORA_A + LORA_G
C_GATE = 2 * D_MODEL
C_IN = C_A + C_RWKV + C_GATE
N_GROUPS = 4
EXPERTS_PER_GROUP = 4
N_EXPERTS = N_GROUPS * EXPERTS_PER_GROUP
D_EXPERT = 256
PLE_DIM = 256
LN_EPS = 1e-5
GN_EPS = 64e-5

LANES = 128
ROUTER_LANES = LANES
RWKV_CHUNK = 64
RWKV_TILE = 256
RWKV_TILES_PER_STEP = 2
OP_R, OP_LW, OP_K, OP_V, OP_KN, OP_IC, OP_G = range(7)
N_OPS = 7
MOE_CAP = 192
PROJ_SUB = 128
VMEM_LIMIT = 56 * 1024 * 1024


def _ln(x, g, b, eps):
    mu = jnp.mean(x, axis=-1, keepdims=True)
    xc = x - mu
    var = jnp.mean(xc * xc, axis=-1, keepdims=True)
    return xc * lax.rsqrt(var + eps) * g + b


def _dot(a, b):
    return jnp.dot(a, b, preferred_element_type=F32)


def _split(x):
    hi = x.astype(BF16)
    lo = (x - hi.astype(F32)).astype(BF16)
    return hi, lo


def _dg(a, b, ca, cb):
    return lax.dot_general(a, b, (((ca,), (cb,)), ((), ())), preferred_element_type=F32)


def _gelu_tanh(x):
    return 0.5 * x * (1.0 + jnp.tanh(0.7978845608028654 * (x + 0.044715 * (x * x * x))))


def _sigmoid(x):
    return 1.0 / (1.0 + jnp.exp(-x))


def _softplus(x):
    return jnp.maximum(x, 0.0) + jnp.log(1.0 + jnp.exp(-jnp.abs(x)))


def _pair_head_sum(z):
    in_first = lax.broadcasted_iota(jnp.int32, (1, LANES), 1) < HEAD_B
    s0 = jnp.sum(jnp.where(in_first, z, 0.0), axis=-1, keepdims=True)
    s1 = jnp.sum(jnp.where(in_first, 0.0, z), axis=-1, keepdims=True)
    return jnp.where(in_first, s0, s1)


def _const_spec(shape):
    nd = len(shape)
    return pl.BlockSpec(shape, lambda *_: (0,) * nd, pipeline_mode=pl.Buffered(1))


def _layer_spec(arr, layer):
    nd = arr.ndim - 1
    return pl.BlockSpec((None,) + arr.shape[1:], lambda *_: (layer,) + (0,) * nd, pipeline_mode=pl.Buffered(1))


def _proj_kernel(*refs, tm, sub, tiles_per_seq, single_token, norm_input):
    if single_token:
        (x_ref, lning_ref, lninb_ref, prev_ref, win_ref, bgate_ref, lnvg_ref, lnvb_ref, wmix_ref, bmix_ref,
         waout_ref, mu_ref, w0_ref, ww2_ref, a0_ref, wa2_ref, wg2_ref, kkw_ref, kaw_ref,
         ga_ref, gb_ref, va_ref, last_ref, ops_ref) = refs
    else:
        (x_ref, lning_ref, lninb_ref, win_ref, bgate_ref, lnvg_ref, lnvb_ref, wmix_ref, bmix_ref,
         waout_ref, mu_ref, w0_ref, ww2_ref, a0_ref, wa2_ref, wg2_ref, kkw_ref, kaw_ref,
         ga_ref, gb_ref, va_ref, last_ref, ops_ref, carry_ref) = refs

    n_sub = tm // sub
    rr = lax.broadcasted_iota(jnp.int32, (CHUNK, CHUNK), 0)
    cc = lax.broadcasted_iota(jnp.int32, (CHUNK, CHUNK), 1)
    wmix = [jnp.where(rr >= cc, wmix_ref[g], 0.0).astype(BF16) for g in range(G_A)]
    if not single_token:
        i = pl.program_id(0)

        @pl.when(i == 0)
        def _():
            carry_ref[...] = jnp.zeros_like(carry_ref)

        first = jnp.where(i % tiles_per_seq == 0, jnp.zeros((1, C_RWKV), F32), carry_ref[0:1, :])
        row = lax.broadcasted_iota(jnp.int32, (sub, C_RWKV), 0)

    def project(q):
        x = x_ref[q * sub:(q + 1) * sub, :]
        if norm_input:
            x = _ln(x, lning_ref[...], lninb_ref[...], LN_EPS)
        xb = x.astype(BF16)
        return (_dot(xb, win_ref[:, 0:C_A]), _dot(xb, win_ref[:, C_A + C_RWKV:]),
                _dot(xb, win_ref[:, C_A:C_A + C_RWKV]))

    def finish(q, pa, pg, pb, prev_row):
        rows = slice(q * sub, (q + 1) * sub)
        z = _gelu_tanh(pa)
        u = z[:, :D_A]
        va = _ln(z[:, D_A:], lnvg_ref[...], lnvb_ref[...], LN_EPS)
        va_ref[rows, :] = va
        vab = va.astype(BF16)
        mix = jnp.concatenate(
            [jnp.concatenate([_dot(wmix[g], vab[c * CHUNK:(c + 1) * CHUNK, g * LANES:(g + 1) * LANES])
                              for g in range(G_A)], axis=1) + bmix_ref[...]
             for c in range(sub // CHUNK)], axis=0)
        ao = _dot((u * mix).astype(BF16), waout_ref[...])
        gates = _sigmoid(pg + bgate_ref[...])
        ga_ref[rows, :] = (gates[:, :D_MODEL] * ao).astype(BF16)
        gb_ref[rows, :] = gates[:, D_MODEL:].astype(BF16)

        if single_token:
            prev = prev_ref[rows, :]
        else:
            prev = jnp.where(row == 0, prev_row, pltpu.roll(pb, 1, 0))
        xs = pb + mu_ref[...] * (prev - pb)
        r = xs[:, 0:D_B]
        k = xs[:, D_B:2 * D_B]
        v = xs[:, 2 * D_B:3 * D_B]
        o = 3 * D_B
        wa_d = xs[:, o:o + LORA_W + LORA_A]
        gd = xs[:, o + LORA_W + LORA_A:]
        w_log = -_softplus(-(w0_ref[...] + _dot(jnp.tanh(wa_d).astype(BF16), ww2_ref[...]))) - 0.5
        iclr = _sigmoid(a0_ref[...] + _dot(wa_d.astype(BF16), wa2_ref[...]))
        kk = k * kkw_ref[...]
        sq = kk * kk
        ss = jnp.concatenate([_pair_head_sum(sq[:, p * LANES:(p + 1) * LANES]) for p in range(D_B // LANES)], axis=1)
        packed = jnp.concatenate(
            [r, -jnp.exp(w_log), k * (1.0 + (iclr - 1.0) * kaw_ref[...]), v,
             kk / jnp.maximum(jnp.sqrt(ss), 1e-12), iclr, _dot(_sigmoid(gd).astype(BF16), wg2_ref[...])], axis=1)
        if single_token:
            ops_ref[:, rows] = packed.T
        else:
            ops_ref[rows, :] = packed

    proj = [project(0)]
    for q in range(n_sub):
        if q + 1 < n_sub:
            proj.append(project(q + 1))
        prev_row = None
        if not single_token:
            prev_row = first if q == 0 else proj[q - 1][2][sub - 1:sub, :]
        finish(q, *proj[q], prev_row)
    pb_last = proj[n_sub - 1][2]
    if single_token:
        last_ref[...] = pb_last
    else:
        carry_ref[0:1, :] = pb_last[sub - 1:sub, :]
        last_ref[0] = pb_last[sub - 8:sub, :]


def _proj_call(x, prev, layer, lw, tm, sub, tiles_per_seq, single_token, norm_input):
    m = x.shape[0]
    nt = m // tm
    row = lambda n: pl.BlockSpec((tm, n), lambda i: (i, 0))
    in_specs = [row(D_MODEL), _const_spec((1, D_MODEL)), _const_spec((1, D_MODEL))]
    args = [x, lw["ln_in_g"], lw["ln_in_b"]]
    if single_token:
        in_specs.append(pl.BlockSpec((None, tm, C_RWKV), lambda i: (layer, i, 0)))
        args.append(prev)
    weights = [lw[n] for n in ("w_in", "b_gate", "ln_v_g", "ln_v_b", "wmix", "bmix", "w_a_out", "mu_shift", "w0",
                               "w_w2p", "a0", "w_a2p", "w_g2", "k_k", "k_a")]
    in_specs += [_layer_spec(w, layer) for w in weights]
    args += weights
    out_shape = [jax.ShapeDtypeStruct((m, D_MODEL), BF16), jax.ShapeDtypeStruct((m, D_MODEL), BF16),
                 jax.ShapeDtypeStruct((m, D_A), F32)]
    out_specs = [row(D_MODEL), row(D_MODEL), row(D_A)]
    if single_token:
        out_shape.append(jax.ShapeDtypeStruct((m, C_RWKV), F32))
        out_specs.append(row(C_RWKV))
    else:
        out_shape.append(jax.ShapeDtypeStruct((nt, 8, C_RWKV), F32))
        out_specs.append(pl.BlockSpec((1, 8, C_RWKV), lambda i: (i, 0, 0)))
    if single_token:
        out_shape.append(jax.ShapeDtypeStruct((N_OPS * D_B, m), F32))
        out_specs.append(pl.BlockSpec((N_OPS * D_B, tm), lambda i: (0, i)))
    else:
        out_shape.append(jax.ShapeDtypeStruct((m, N_OPS * D_B), F32))
        out_specs.append(row(N_OPS * D_B))
    scratch = [] if single_token else [pltpu.VMEM((8, C_RWKV), F32)]
    return pl.pallas_call(
        functools.partial(_proj_kernel, tm=tm, sub=sub, tiles_per_seq=tiles_per_seq, single_token=single_token,
                          norm_input=norm_input),
        grid=(nt,),
        in_specs=in_specs,
        out_specs=out_specs,
        out_shape=out_shape,
        scratch_shapes=scratch,
        compiler_params=pltpu.CompilerParams(dimension_semantics=("arbitrary",), vmem_limit_bytes=VMEM_LIMIT),
        name="proj_single" if single_token else "proj_seq",
    )(*args)


def _rwkv_tile_kernel(ops_ref, lnxg_ref, lnxb_ref, rkw_ref, y_ref, sfin_ref, state_ref, *, n_tile):
    ti = pl.program_id(1)
    n_t = pl.num_programs(1)
    C = RWKV_CHUNK
    RT = RWKV_TILE
    n_ch = RT // C
    n_pair = H_B // 2

    @pl.when(ti == 0)
    def _():
        state_ref[...] = jnp.zeros_like(state_ref)

    row = lax.broadcasted_iota(jnp.int32, (RT, RT), 0)
    col = lax.broadcasted_iota(jnp.int32, (RT, RT), 1)
    same = (row // C) == (col // C)
    incl = jnp.logical_and(same, row >= col)
    strict = jnp.logical_and(same, row > col)
    row_in_chunk = lax.broadcasted_iota(jnp.int32, (RT, 1), 0) % C
    eye = jnp.where(row == col, 1.0, 0.0)
    prow = lax.broadcasted_iota(jnp.int32, (LANES, LANES), 0)
    pcol = lax.broadcasted_iota(jnp.int32, (LANES, LANES), 1)
    pair_bd = (prow // HEAD_B) == (pcol // HEAD_B)
    in_first = lax.broadcasted_iota(jnp.int32, (1, LANES), 1) < HEAD_B
    first_head = jnp.where(in_first, 1.0, 0.0)
    second_head = 1.0 - first_head
    sls = [slice(p * LANES, (p + 1) * LANES) for p in range(n_pair)]
    pr = [h // 2 for h in range(H_B)]

    def op(j, rows, sl):
        return ops_ref[rows, j * D_B + sl.start:j * D_B + sl.stop]

    def prologue(t):
        rows = slice(t * RT, (t + 1) * RT)
        lw = op(OP_LW, rows, slice(0, D_B))
        cum = lw
        for shift in (1, 2, 4, 8, 16, 32):
            cum = cum + jnp.where(row_in_chunk >= shift, pltpu.roll(cum, shift, 0), 0.0)
        ends = [cum[(c + 1) * C - 1:(c + 1) * C, :] for c in range(n_ch)]
        cum_end = jnp.concatenate([jnp.broadcast_to(e, (C, D_B)) for e in ends], axis=0)
        d = dict(rows=rows, ends=ends, atm=[], rtm=[], rt=[], bt=[], kt=[], bh=[], kh=[], v=[])
        for p in range(n_pair):
            sl = sls[p]
            cum_p = cum[:, sl]
            kn_p = op(OP_KN, rows, sl)
            b_p = kn_p * op(OP_IC, rows, sl)
            k_p = op(OP_K, rows, sl)
            e_inv = jnp.exp(-cum_p)
            e_hat = jnp.exp(cum_end[:, sl] - cum_p)
            at_p = -kn_p * jnp.exp(cum_p - lw[:, sl])
            rt_p = op(OP_R, rows, sl) * jnp.exp(cum_p)
            for hm in (first_head, second_head):
                d["atm"].append((at_p * hm).astype(BF16))
                d["rtm"].append((rt_p * hm).astype(BF16))
            d["rt"].append(rt_p)
            d["bt"].append((b_p * e_inv).astype(BF16))
            d["kt"].append((k_p * e_inv).astype(BF16))
            d["bh"].append((b_p * e_hat).astype(BF16))
            d["kh"].append((k_p * e_hat).astype(BF16))
            d["v"].append(op(OP_V, rows, sl).astype(BF16))
        return d

    def main(d):
        atm, rtm, bt, kt, v = d["atm"], d["rtm"], d["bt"], d["kt"], d["v"]
        xs = [jnp.where(strict, _dg(atm[h], bt[pr[h]], 1, 1), 0.0) for h in range(H_B)]
        yield
        ts = [eye + x for x in xs]
        xb = [x.astype(BF16) for x in xs]
        for _ in range(5):
            xb = [_dot(z, z).astype(BF16) for z in xb]
            yield
            ts = [t + _dot(t.astype(BF16), z) for t, z in zip(ts, xb)]
            yield
        tb = [t.astype(BF16) for t in ts]
        a_ak = [jnp.where(strict, _dg(atm[h], kt[pr[h]], 1, 1), 0.0).astype(BF16) for h in range(H_B)]
        yield
        av = [_dot(a_ak[h], v[pr[h]]).astype(BF16) for h in range(H_B)]
        yield
        tx = [_dot(tb[h], jnp.concatenate([av[h], atm[h]], axis=1)) for h in range(H_B)]
        yield
        a_rb = [jnp.where(incl, _dg(rtm[h], bt[pr[h]], 1, 1), 0.0).astype(BF16) for h in range(H_B)]
        yield
        a_rk = [jnp.where(incl, _dg(rtm[h], kt[pr[h]], 1, 1), 0.0).astype(BF16) for h in range(H_B)]
        yield
        rb = [_dot(a_rb[h], tx[h].astype(BF16)) for h in range(H_B)]
        yield
        rkv = [_dot(a_rk[h], v[pr[h]]) for h in range(H_B)]
        d["w2"] = [jnp.where(in_first, tx[2 * p][:, :LANES], tx[2 * p + 1][:, :LANES]).astype(BF16)
                   for p in range(n_pair)]
        d["a_t"] = [(tx[2 * p][:, LANES:] + tx[2 * p + 1][:, LANES:]).astype(BF16) for p in range(n_pair)]
        d["rhat"] = [(d["rt"][p] + (rb[2 * p][:, LANES:] + rb[2 * p + 1][:, LANES:])).astype(BF16)
                     for p in range(n_pair)]
        d["y0"] = [jnp.where(in_first, rb[2 * p][:, :LANES] + rkv[2 * p], rb[2 * p + 1][:, :LANES] + rkv[2 * p + 1])
                   for p in range(n_pair)]
        yield

    def tail(d, st):
        w2, a_t, rhat, y0, bh, kh, v = d["w2"], d["a_t"], d["rhat"], d["y0"], d["bh"], d["kh"], d["v"]
        ys = [[] for _ in range(n_pair)]
        for c in range(n_ch):
            rows = slice(c * C, (c + 1) * C)
            el = [jnp.where(pair_bd, _dg(a_t[p][rows], bh[p][rows], 0, 0), 0.0).astype(BF16) for p in range(n_pair)]
            gm = [jnp.where(pair_bd,
                            _dg(jnp.concatenate([w2[p][rows], v[p][rows]], axis=0),
                                jnp.concatenate([bh[p][rows], kh[p][rows]], axis=0), 0, 0), 0.0)
                  for p in range(n_pair)]
            s = st["s"]
            sb = [z.astype(BF16) for z in s]
            for p in range(n_pair):
                ys[p].append(_dg(rhat[p][rows], sb[p], 1, 1) + y0[p][rows])
            st["s"] = [s[p] * jnp.exp(d["ends"][c][:, sls[p]]) + _dot(sb[p], el[p]) + gm[p] for p in range(n_pair)]
            yield
        rows = d["rows"]
        for p in range(n_pair):
            sl = sls[p]
            y = jnp.concatenate(ys[p], axis=0)
            mu = _pair_head_sum(y) * (1.0 / HEAD_B)
            yc = y - mu
            var = _pair_head_sum(yc * yc) * (1.0 / HEAD_B)
            yn = yc * lax.rsqrt(var + GN_EPS) * lnxg_ref[:, sl] + lnxb_ref[:, sl]
            bonus = (_pair_head_sum(op(OP_R, rows, sl) * op(OP_K, rows, sl) * rkw_ref[:, sl])
                     * op(OP_V, rows, sl))
            y_ref[rows, sl] = (yn + bonus) * op(OP_G, rows, sl)
            yield

    def drain(gen):
        for _ in gen:
            pass

    st = {"s": [state_ref[p] for p in range(n_pair)]}
    pro = prologue(0)
    stages = main(pro)
    next(stages)
    prev_tail = None
    for t in range(n_tile):
        nxt = prologue(t + 1) if t + 1 < n_tile else None
        for n, _ in enumerate(stages):
            if prev_tail is not None and n % 2 == 1:
                next(prev_tail, None)
        if prev_tail is not None:
            drain(prev_tail)
        prev_tail = tail(pro, st)
        if nxt is not None:
            pro = nxt
            stages = main(pro)
            next(stages)
    drain(prev_tail)
    for p in range(n_pair):
        state_ref[p] = st["s"][p]

    @pl.when(ti == n_t - 1)
    def _():
        sfin_ref[0] = state_ref[...]


def _rwkv_seq_call(ops, layer, lw, batch, seq):
    RT = RWKV_TILE * RWKV_TILES_PER_STEP
    n_t = seq // RT
    n_pair = H_B // 2
    blk = pl.BlockSpec((RT, D_B), lambda b, t: (b * n_t + t, 0))
    vec = pl.BlockSpec((None, 1, D_B), lambda b, t: (layer, 0, 0))
    y, s_bd = pl.pallas_call(
        functools.partial(_rwkv_tile_kernel, n_tile=RWKV_TILES_PER_STEP),
        grid=(batch, n_t),
        in_specs=[pl.BlockSpec((RT, N_OPS * D_B), lambda b, t: (b * n_t + t, 0))] + [vec] * 3,
        out_specs=[blk, pl.BlockSpec((1, n_pair, LANES, LANES), lambda b, t: (b, 0, 0, 0))],
        out_shape=[jax.ShapeDtypeStruct((batch * seq, D_B), F32),
                   jax.ShapeDtypeStruct((batch, n_pair, LANES, LANES), F32)],
        scratch_shapes=[pltpu.VMEM((n_pair, LANES, LANES), F32)],
        compiler_params=pltpu.CompilerParams(dimension_semantics=("arbitrary", "arbitrary"),
                                             vmem_limit_bytes=VMEM_LIMIT),
        name="rwkv_seq",
    )(ops, lw["lnx_g"], lw["lnx_b"], lw["r_k"])
    s_bd = s_bd.reshape(batch, n_pair, 2, HEAD_B, 2, HEAD_B)
    s_fin = jnp.stack([s_bd[:, :, 0, :, 0, :], s_bd[:, :, 1, :, 1, :]], axis=2)
    return y, s_fin.reshape(batch, H_B, HEAD_B, HEAD_B)


def _rwkv_step_kernel(s_ref, r_ref, lw_ref, k_ref, v_ref, kn_ref, ic_ref, g_ref, lnxg_ref, lnxb_ref, rkw_ref,
                      buf_ref, y_ref, so_ref, y_scr):
    del buf_ref
    kn = kn_ref[...]
    a = -kn
    b = kn * ic_ref[...]
    w = jnp.exp(lw_ref[...])
    k = k_ref[...]
    r = r_ref[...]
    for i in range(HEAD_B):
        s = s_ref[i]
        sa = jnp.sum(s * a, axis=0, keepdims=True)
        s1 = s * w + sa * b + v_ref[i:i + 1, :] * k
        so_ref[i] = s1
        y_scr[i:i + 1, :] = jnp.sum(s1 * r, axis=0, keepdims=True)
    y = y_scr[...]
    mu = jnp.mean(y, axis=0, keepdims=True)
    yc = y - mu
    var = jnp.mean(yc * yc, axis=0, keepdims=True)
    yn = yc * lax.rsqrt(var + GN_EPS) * lnxg_ref[...] + lnxb_ref[...]
    bonus = jnp.sum(r * k * rkw_ref[...], axis=0, keepdims=True) * v_ref[...]
    y_ref[...] = (yn + bonus) * g_ref[...]


def _rwkv_step_call(state_t, buf, layer, ops_t, lw, nb):
    sblk = pl.BlockSpec((None, None, HEAD_B, HEAD_B, nb), lambda h: (layer, h, 0, 0, 0))
    vblk = pl.BlockSpec((HEAD_B, nb), lambda h: (h, 0))
    pblk = pl.BlockSpec((None, HEAD_B, nb), lambda h: (layer, h, 0))
    return pl.pallas_call(
        _rwkv_step_kernel,
        grid=(H_B,),
        in_specs=[sblk] + [pl.BlockSpec((HEAD_B, nb), lambda h, j=j: (j * H_B + h, 0)) for j in range(N_OPS)]
                 + [pblk] * 3 + [pl.BlockSpec(memory_space=pl.ANY)],
        out_specs=[vblk, sblk],
        out_shape=[jax.ShapeDtypeStruct((D_B, nb), F32), jax.ShapeDtypeStruct(buf.shape, F32)],
        scratch_shapes=[pltpu.VMEM((HEAD_B, nb), F32)],
        input_output_aliases={11: 1},
        compiler_params=pltpu.CompilerParams(dimension_semantics=("arbitrary",)),
        name="rwkv_step",
    )(state_t, *([ops_t] * N_OPS), lw["lnx_g_t"], lw["lnx_b_t"], lw["r_k_t"], buf)


def _route(lg, tm):
    lane = lax.broadcasted_iota(jnp.int32, (tm, ROUTER_LANES), 1)
    lanef = lane.astype(F32)
    big = jnp.float32(1e9)
    ninf = jnp.float32(-jnp.inf)
    isg = lane < N_GROUPS
    lgm = jnp.where(isg, lg, ninf)
    gmax = jnp.max(lgm, axis=-1, keepdims=True)
    den = jnp.sum(jnp.exp(lgm - gmax), axis=-1, keepdims=True)
    gp = 1.0 / den
    gi = jnp.min(jnp.where(lgm == gmax, lanef, big), axis=-1, keepdims=True)
    grp = ((lane - N_GROUPS) >> 2).astype(F32)
    ise = jnp.where(lane >= N_GROUPS, grp, big) == gi
    le = jnp.where(ise, lg, ninf)
    m1 = jnp.max(le, axis=-1, keepdims=True)
    i1 = jnp.min(jnp.where(le == m1, lanef, big), axis=-1, keepdims=True)
    le2 = jnp.where(lanef == i1, ninf, le)
    m2 = jnp.max(le2, axis=-1, keepdims=True)
    i2 = jnp.min(jnp.where(le2 == m2, lanef, big), axis=-1, keepdims=True)
    e2 = jnp.exp(m2 - m1)
    w1 = 1.0 / (1.0 + e2)
    w2 = e2 / (1.0 + e2)
    return jnp.where(lanef == i1, w1 * gp, 0.0) + jnp.where(lanef == i2, w2 * gp, 0.0), gi


def _group_experts(xb, cw, grp, weg_ref, weu_ref, wed_ref):
    hs = []
    for j in range(EXPERTS_PER_GROUP):
        e = grp * EXPERTS_PER_GROUP + j
        hg = _dot(xb, weg_ref[e])
        hu = _dot(xb, weu_ref[e])
        ce = cw[:, N_GROUPS + e:N_GROUPS + e + 1]
        hs.append(((hg * _sigmoid(hg)) * hu * ce).astype(BF16))
    rows = slice(grp * EXPERTS_PER_GROUP * D_EXPERT, (grp + 1) * EXPERTS_PER_GROUP * D_EXPERT)
    return _dot(jnp.concatenate(hs, axis=1), wed_ref[rows, :])


def _experts_dense(x1b, comb, weg_ref, weu_ref, wed_ref):
    moe = _group_experts(x1b, comb, 0, weg_ref, weu_ref, wed_ref)
    for grp in range(1, N_GROUPS):
        moe = moe + _group_experts(x1b, comb, grp, weg_ref, weu_ref, wed_ref)
    return moe


def _experts_sorted(x1b, comb, gi, onehot, tm, weg_ref, weu_ref, wed_ref):
    r = lax.broadcasted_iota(jnp.int32, (tm, tm), 0)
    c = lax.broadcasted_iota(jnp.int32, (tm, tm), 1)
    earlier = jnp.where(r > c, 1.0, 0.0).astype(BF16)
    rank = jnp.sum(onehot * _dot(earlier, onehot.astype(BF16)), axis=-1, keepdims=True)
    key = rank + float(tm) * gi
    key_row = jnp.transpose(jnp.broadcast_to(key, (tm, LANES)))[0:1, :]
    slot = lax.broadcasted_iota(jnp.int32, (MOE_CAP, 1), 0).astype(F32)
    ch, cl = _split(comb)
    sels, outs = [], []
    for grp in range(N_GROUPS):
        sel = jnp.where(key_row == slot + float(tm * grp), 1.0, 0.0).astype(BF16)
        xg = _dot(sel, x1b).astype(BF16)
        cg = _dot(sel, ch) + _dot(sel, cl)
        sels.append(sel)
        outs.append(_group_experts(xg, cg, grp, weg_ref, weu_ref, wed_ref).astype(BF16))
    return _dg(jnp.concatenate(sels, axis=0), jnp.concatenate(outs, axis=0), 0, 0)


def _mix_kernel(x_ref, lning_ref, lninb_ref, ga_ref, gb_ref, yb_ref, pe_ref, wbout_ref, wo_ref, ln1g_ref,
                ln1b_ref, wr_ref, br_ref, weg_ref, weu_ref, wed_ref, wpg_ref, wpp_ref, ln2g_ref, ln2b_ref,
                o_ref, *scratch, tm, dn_alpha, norm_input, yb_on_lanes):
    x = x_ref[...]
    if norm_input:
        x = _ln(x, lning_ref[...], lninb_ref[...], LN_EPS)
    if yb_on_lanes:
        bo = _dg(yb_ref[...].astype(BF16), wbout_ref[...], 0, 0)
    else:
        bo = _dot(yb_ref[...].astype(BF16), wbout_ref[...])
    merged = ga_ref[...] + gb_ref[...] * bo
    mix = _dot(merged.astype(BF16), wo_ref[...])
    x1 = _ln(dn_alpha * x + mix, ln1g_ref[...], ln1b_ref[...], LN_EPS)
    x1b = x1.astype(BF16)

    x1h, x1l = _split(x1)
    wrh = wr_ref[0]
    wrl = wr_ref[1]
    lg = _dot(x1h, wrh) + (_dot(x1h, wrl) + _dot(x1l, wrh)) + br_ref[...]
    comb, gi = _route(lg, tm)

    if scratch:
        (moe_ref,) = scratch
        lane = lax.broadcasted_iota(jnp.int32, (tm, ROUTER_LANES), 1).astype(F32)
        onehot = jnp.where(lane == gi, 1.0, 0.0)
        fits = jnp.max(jnp.sum(onehot, axis=0, keepdims=True)) <= MOE_CAP

        @pl.when(fits)
        def _():
            moe_ref[...] = _experts_sorted(x1b, comb, gi, onehot, tm, weg_ref, weu_ref, wed_ref)

        @pl.when(jnp.logical_not(fits))
        def _():
            moe_ref[...] = _experts_dense(x1b, comb, weg_ref, weu_ref, wed_ref)

        moe = moe_ref[...]
    else:
        moe = _experts_dense(x1b, comb, weg_ref, weu_ref, wed_ref)

    ple = _sigmoid(_dot(x1b, wpg_ref[...])) * _dot(pe_ref[...].astype(BF16), wpp_ref[...])
    o_ref[...] = _ln(dn_alpha * x1 + moe + ple, ln2g_ref[...], ln2b_ref[...], LN_EPS)


def _mix_call(x, ga, gb, yb, pe, layer, lw, tm, dn_alpha, norm_input, yb_on_lanes):
    m = x.shape[0]
    row = lambda n: pl.BlockSpec((tm, n), lambda i: (i, 0))
    yb_spec = pl.BlockSpec((D_B, tm), lambda i: (0, i)) if yb_on_lanes else row(D_B)
    weights = [lw[n] for n in ("w_b_out", "w_o", "ln1_g", "ln1_b", "w_router", "b_router", "w_e_gate", "w_e_up",
                               "w_e_down", "w_pe_gate", "w_pe_proj", "ln2_g", "ln2_b")]
    return pl.pallas_call(
        functools.partial(_mix_kernel, tm=tm, dn_alpha=dn_alpha, norm_input=norm_input, yb_on_lanes=yb_on_lanes),
        grid=(m // tm,),
        in_specs=[row(D_MODEL), _const_spec((1, D_MODEL)), _const_spec((1, D_MODEL)), row(D_MODEL), row(D_MODEL),
                  yb_spec, pl.BlockSpec((None, tm, PLE_DIM), lambda i: (layer, i, 0))]
                 + [_layer_spec(w, layer) for w in weights],
        out_specs=row(D_MODEL),
        out_shape=jax.ShapeDtypeStruct((m, D_MODEL), F32),
        scratch_shapes=[pltpu.VMEM((tm, D_MODEL), F32)] if tm > MOE_CAP else [],
        compiler_params=pltpu.CompilerParams(dimension_semantics=("parallel",), vmem_limit_bytes=VMEM_LIMIT),
        name="mix_moe",
    )(x, lw["ln_in_g"], lw["ln_in_b"], ga, gb, yb, pe, *weights)


def _prep_weights(p, nb):
    depth = p["w_in"].shape[0]
    vec = lambda z: z.reshape(depth, 1, -1)
    bf = lambda z: z.astype(BF16)
    lanes = lambda z: jnp.broadcast_to(z.reshape(depth, D_B, 1), (depth, D_B, nb))
    zeros64 = jnp.zeros((depth, LORA_W, D_B), F32)
    pad = ROUTER_LANES - N_GROUPS - N_EXPERTS
    w_router = jnp.concatenate(
        [p["w_rg"], jnp.transpose(p["w_re"], (0, 2, 1, 3)).reshape(depth, D_MODEL, N_EXPERTS),
         jnp.zeros((depth, D_MODEL, pad), F32)], axis=2)
    wr_hi = w_router.astype(BF16)
    wr_lo = (w_router - wr_hi.astype(F32)).astype(BF16)
    b_router = jnp.concatenate([p["b_rg"], p["b_re"].reshape(depth, N_EXPERTS), jnp.zeros((depth, pad), F32)], axis=1)
    eye = jnp.eye(CHUNK, dtype=F32)
    w = {n: vec(p[n]) for n in ("b_gate", "ln_v_g", "ln_v_b", "mu_shift", "w0", "a0", "k_k", "k_a", "lnx_g", "lnx_b",
                                "ln1_g", "ln1_b", "ln2_g", "ln2_b")}
    w.update({n: bf(p[n]) for n in ("w_in", "w_a_out", "w_g2", "w_b_out", "w_o", "w_e_gate", "w_e_up", "w_pe_gate",
                                    "w_pe_proj")})
    w.update({
        "w_w2p": bf(jnp.concatenate([p["w_w2"], zeros64], axis=1)),
        "w_a2p": bf(jnp.concatenate([zeros64, p["w_a2"]], axis=1)),
        "r_k": vec(p["r_k"]),
        "lnx_g_t": lanes(p["lnx_g"]), "lnx_b_t": lanes(p["lnx_b"]), "r_k_t": lanes(p["r_k"]),
        "w_router": jnp.stack([wr_hi, wr_lo], axis=1), "b_router": vec(b_router),
        "w_e_down": bf(p["w_e_down"]).reshape(depth, N_EXPERTS * D_EXPERT, D_MODEL),
        "wmix_seq": p["w_s"],
        "bmix_seq": jnp.repeat(jnp.transpose(p["b_s"], (0, 2, 1)), LANES, axis=2),
        "wmix_one": p["w_s"][:, :, 0, 0][:, :, None, None] * eye,
        "bmix_one": jnp.broadcast_to(jnp.repeat(p["b_s"][:, :, 0], LANES, axis=1)[:, None, :], (depth, CHUNK, D_A)),
    })
    return w


def kernel(x_prompt, x_sample, state_rwkv, state_shift, p_prompt, p_sample, ln_in_g, ln_in_b, w_in, b_gate,
           ln_v_g, ln_v_b, w_s, b_s, w_a_out, mu_shift, w0, w_w2, a0, w_a2, w_g2, k_k, k_a, r_k, lnx_g, lnx_b,
           w_b_out, w_o, ln1_g, ln1_b, w_rg, b_rg, w_re, b_re, w_e_gate, w_e_up, w_e_down, w_pe_gate,
           w_pe_proj, ln2_g, ln2_b):
    p = dict(w_in=w_in, b_gate=b_gate, ln_v_g=ln_v_g, ln_v_b=ln_v_b, w_s=w_s, b_s=b_s, w_a_out=w_a_out,
             mu_shift=mu_shift, w0=w0, w_w2=w_w2, a0=a0, w_a2=w_a2, w_g2=w_g2, k_k=k_k, k_a=k_a, r_k=r_k,
             lnx_g=lnx_g, lnx_b=lnx_b, w_b_out=w_b_out, w_o=w_o, ln1_g=ln1_g, ln1_b=ln1_b, w_rg=w_rg, b_rg=b_rg,
             w_re=w_re, b_re=b_re, w_e_gate=w_e_gate, w_e_up=w_e_up, w_e_down=w_e_down,
             w_pe_gate=w_pe_gate, w_pe_proj=w_pe_proj, ln2_g=ln2_g, ln2_b=ln2_b)
    batch, seq, _ = x_prompt.shape
    nb, dec_seq, _ = x_sample.shape
    depth = w_in.shape[0]
    assert dec_seq == 1 and seq % CHUNK == 0 and seq % (RWKV_TILE * RWKV_TILES_PER_STEP) == 0 and nb % LANES == 0
    dn_alpha = (2 * depth) ** 0.25
    tm = 512
    tms = nb
    tiles_per_seq = seq // tm
    last_chunk = ((seq - 1) // CHUNK) * CHUNK

    xp = x_prompt.reshape(batch * seq, D_MODEL)
    xs = x_sample.reshape(nb, D_MODEL)
    pe_p = p_prompt.reshape(depth, batch * seq, PLE_DIM)
    pe_s = p_sample.reshape(depth, nb, PLE_DIM)
    lw = _prep_weights(p, nb)
    lw["ln_in_g"] = ln_in_g.reshape(1, D_MODEL)
    lw["ln_in_b"] = ln_in_b.reshape(1, D_MODEL)
    lw_p = dict(lw, wmix=lw["wmix_seq"], bmix=lw["bmix_seq"])
    lw_s = dict(lw, wmix=lw["wmix_one"], bmix=lw["bmix_one"])
    state_t = jnp.transpose(state_rwkv, (0, 2, 3, 4, 1))
    state_out = jnp.zeros_like(state_t)

    outs = [[] for _ in range(5)]
    for i in range(depth):
        first = i == 0
        res = _proj_call(xp, None, i, lw_p, tm, PROJ_SUB, tiles_per_seq, False, first)
        ga, gb, va, last, ops = res
        yb, s_fin = _rwkv_seq_call(ops, i, lw, batch, seq)
        xp = _mix_call(xp, ga, gb, yb, pe_p, i, lw, tm, dn_alpha, first, False)
        outs[0].append(s_fin)
        outs[1].append(last[tiles_per_seq - 1::tiles_per_seq, 7, :])
        outs[2].append(va.reshape(batch, seq, D_A)[:, last_chunk:])

        res = _proj_call(xs, state_shift, i, lw_s, tms, tms, 1, True, first)
        ga, gb, va, pb, ops_t = res
        yb_t, state_out = _rwkv_step_call(state_t, state_out, i, ops_t, lw, nb)
        xs = _mix_call(xs, ga, gb, yb_t, pe_s, i, lw, tms, dn_alpha, first, True)
        outs[3].append(pb)
        outs[4].append(va.reshape(nb, 1, D_A))

    return (xp.reshape(batch, seq, D_MODEL), xs.reshape(nb, 1, D_MODEL),
            jnp.stack(outs[0]), jnp.stack(outs[1]), jnp.stack(outs[2]),
            jnp.transpose(state_out, (0, 4, 1, 2, 3)), jnp.stack(outs[3]), jnp.stack(outs[4]))
```
